```python
import math
import jax, jax.numpy as jnp
from jax import lax
import numpy as np

D_MODEL = 1024
BATCH = 2
SEQ = 8192
DEPTH = 1
DEC_BATCH = 1
DEC_SEQ = 16384
PAST_LEN = 128

D_RNN = 1024
LRU_BLOCKS = 16
LRU_BLOCK = D_RNN // LRU_BLOCKS
CONV_WIDTH = 4
CONV_PAD = (2, 1)
LRU_C = 8.0
N_HEADS = 8
HEAD_DIM = 64
V_DIM = 2 * HEAD_DIM
QK_WIDTH = N_HEADS * 2 * HEAD_DIM
D_ATTN = N_HEADS * V_DIM
Q_BLOCK = 128
N_BUCKETS = 32
MAX_DISTANCE = 128
IN_COLS = 2 * D_RNN + 2 * QK_WIDTH + D_ATTN + 2 * D_MODEL
N_EXPERTS = 64
TOP_K = 8
N_GROUPS = 8
TOPK_GROUPS = 4
D_EXPERT = 256
D_SHARED = 256
ROUTED_SCALE = 2.5
EXPERT_BLOCK = 128
EPS = 1e-6

kernel_name = "hybrid_rglru_diffattn_moe_encoder"


def rms_norm(x, g):
    xf = x.astype(jnp.float32)
    y = xf * lax.rsqrt(jnp.mean(xf * xf, axis=-1, keepdims=True) + EPS)
    return (y * g.astype(jnp.float32)).astype(x.dtype)


def depthwise_conv(u, w, b):
    out = lax.conv_general_dilated(
        u, w[:, None, :].astype(u.dtype), window_strides=(1,), padding=[CONV_PAD],
        dimension_numbers=('NWC', 'WIO', 'NWC'), feature_group_count=D_RNN)
    return out + b.astype(u.dtype)


def block_diag(x, w):
    xb = x.reshape(x.shape[:-1] + (LRU_BLOCKS, LRU_BLOCK))
    return jnp.einsum('blnc,ncd->blnd', xb, w).reshape(x.shape)


def _lin_combine(e1, e2):
    a1, b1 = e1
    a2, b2 = e2
    return a1 * a2, a2 * b1 + b2


def rg_lru_dir(xf, w_r, b_r, w_i, b_i, lam, reverse):
    r = jax.nn.sigmoid(block_diag(xf, w_r.astype(jnp.float32)) + b_r.astype(jnp.float32))
    i = jax.nn.sigmoid(block_diag(xf, w_i.astype(jnp.float32)) + b_i.astype(jnp.float32))
    log_a = -LRU_C * r * jax.nn.softplus(-lam.astype(jnp.float32))
    a = jnp.exp(log_a)
    b = jnp.sqrt(-jnp.expm1(2.0 * log_a)) * (i * xf)
    _, h = lax.associative_scan(_lin_combine, (a, b), axis=1, reverse=reverse)
    return h


def t5_bucket(rel):
    half = N_BUCKETS // 2
    max_exact = half // 2
    ret = jnp.where(rel > 0, half, 0)
    n = jnp.abs(rel)
    nf = jnp.maximum(n, 1).astype(jnp.float32)
    large = max_exact + (jnp.log(nf / max_exact) / math.log(MAX_DISTANCE / max_exact)
                         * (half - max_exact)).astype(jnp.int32)
    large = jnp.minimum(large, half - 1)
    return ret + jnp.where(n < max_exact, n, large)


def diff_attention(q, k, v, lam, lam_init, subln_g, rel_bias):
    B, L = q.shape[0], q.shape[1]
    nblk = L // Q_BLOCK
    scale = HEAD_DIM ** -0.5
    k1 = k[..., 0, :]
    k2 = k[..., 1, :]
    qb = (q * scale).reshape(B, nblk, Q_BLOCK, N_HEADS, 2, HEAD_DIM).swapaxes(0, 1)
    key_pos = jnp.arange(L, dtype=jnp.int32)

    def block(args):
        q_blk, start = args
        q_pos = start + jnp.arange(Q_BLOCK, dtype=jnp.int32)
        bias = rel_bias[t5_bucket(key_pos[None, :] - q_pos[:, None])]
        bias = bias.astype(jnp.float32).transpose(2, 0, 1)[None]
        s1 = jnp.einsum('bqhd,bkhd->bhqk', q_blk[..., 0, :], k1,
                        preferred_element_type=jnp.float32) + bias
        s2 = jnp.einsum('bqhd,bkhd->bhqk', q_blk[..., 1, :], k2,
                        preferred_element_type=jnp.float32) + bias
        attn = jax.nn.softmax(s1, axis=-1) - lam * jax.nn.softmax(s2, axis=-1)
        return jnp.einsum('bhqk,bkhe->bqhe', attn.astype(v.dtype), v)

    starts = jnp.arange(nblk, dtype=jnp.int32) * Q_BLOCK
    o = lax.map(block, (qb, starts))
    o = o.swapaxes(0, 1).reshape(B, L, N_HEADS, V_DIM)
    o = rms_norm(o, subln_g) * (1.0 - lam_init)
    return o.reshape(B, L, D_ATTN)


def swiglu(x, w1, w3, w2):
    return jnp.dot(jax.nn.silu(jnp.dot(x, w1)) * jnp.dot(x, w3), w2)


def route(h, w_router, router_bias):
    N = h.shape[0]
    s = jax.nn.sigmoid(jnp.dot(h, w_router, preferred_element_type=jnp.float32))
    sb = s + router_bias.astype(jnp.float32)
    grp = sb.reshape(N, N_GROUPS, N_EXPERTS // N_GROUPS)
    gscore = jnp.sum(lax.top_k(grp, 2)[0], axis=-1)
    _, gidx = lax.top_k(gscore, TOPK_GROUPS)
    gmask = jnp.any(gidx[..., None] == jnp.arange(N_GROUPS), axis=-2)
    emask = jnp.repeat(gmask, N_EXPERTS // N_GROUPS, axis=-1)
    _, idx = lax.top_k(jnp.where(emask, sb, -jnp.inf), TOP_K)
    w = jnp.take_along_axis(s, idx, axis=-1)
    w = w / jnp.sum(w, axis=-1, keepdims=True) * ROUTED_SCALE
    return idx, w


def routed_experts(x, idx, wts, w1, w3, w2):
    N, D = x.shape
    A = N * TOP_K
    e_flat = idx.reshape(-1).astype(jnp.int32)
    tok = (jnp.arange(A, dtype=jnp.int32) // TOP_K)
    wt = wts.reshape(-1).astype(jnp.float32)
    order = jnp.argsort(e_flat)
    e_sorted = e_flat[order]
    counts = jnp.zeros((N_EXPERTS,), jnp.int32).at[e_flat].add(1)
    starts = jnp.cumsum(counts) - counts
    pcounts = ((counts + EXPERT_BLOCK - 1) // EXPERT_BLOCK) * EXPERT_BLOCK
    pends = jnp.cumsum(pcounts)
    pstarts = pends - pcounts
    dest = pstarts[e_sorted] + (jnp.arange(A, dtype=jnp.int32) - starts[e_sorted])
    n_blocks = (A + N_EXPERTS * (EXPERT_BLOCK - 1) + EXPERT_BLOCK - 1) // EXPERT_BLOCK
    P = n_blocks * EXPERT_BLOCK
    slot_tok = jnp.full((P,), N, jnp.int32).at[dest].set(tok[order])
    slot_wt = jnp.zeros((P,), jnp.float32).at[dest].set(wt[order])
    blk_start = jnp.arange(n_blocks, dtype=jnp.int32) * EXPERT_BLOCK
    blk_exp = jnp.minimum(jnp.searchsorted(pends, blk_start, side='right'), N_EXPERTS - 1)
    x_pad = jnp.concatenate([x, jnp.zeros((1, D), x.dtype)], axis=0)

    def block(args):
        toks, e = args
        xb = x_pad[toks]
        return swiglu(xb, w1[e], w3[e], w2[e])

    out = lax.map(block, (slot_tok.reshape(n_blocks, EXPERT_BLOCK), blk_exp))
    y = jnp.zeros((N + 1, D), jnp.float32).at[slot_tok].add(
        out.reshape(P, D).astype(jnp.float32) * slot_wt[:, None])
    return y[:N].astype(x.dtype)


def encoder_layer(x, c, layer_idx, norm_mix_g, norm_ffn_g, w_ada, b_ada, w_in, conv_w, conv_b,
                  w_rgate, b_rgate, w_igate, b_igate, lru_lambda, w_rnn_out,
                  lambda_q1, lambda_k1, lambda_q2, lambda_k2, subln_g, rel_bias, w_attn_out, w_o,
                  w_router, router_bias, w1_e, w3_e, w2_e, w1_s, w3_s, w2_s):
    B, L, D = x.shape
    mod = (jnp.dot(jax.nn.silu(c), w_ada) + b_ada)[:, None, :]
    sh_a, sc_a, g_a = mod[..., 0:D], mod[..., D:2 * D], mod[..., 2 * D:3 * D]
    sh_f, sc_f, g_f = mod[..., 3 * D:4 * D], mod[..., 4 * D:5 * D], mod[..., 5 * D:6 * D]

    h = rms_norm(x, norm_mix_g) * (1.0 + sc_a) + sh_a
    proj = jnp.einsum('bld,dc->blc', h, w_in)
    o0 = D_RNN; o1 = o0 + D_RNN; o2 = o1 + QK_WIDTH; o3 = o2 + QK_WIDTH; o4 = o3 + D_ATTN
    u_rnn, u_gate = proj[..., :o0], proj[..., o0:o1]
    q = proj[..., o1:o2].reshape(B, L, N_HEADS, 2, HEAD_DIM)
    k = proj[..., o2:o3].reshape(B, L, N_HEADS, 2, HEAD_DIM)
    v = proj[..., o3:o4].reshape(B, L, N_HEADS, V_DIM)
    g_merge = jax.nn.sigmoid(proj[..., o4:].astype(jnp.float32)).astype(x.dtype)

    xc = depthwise_conv(u_rnn, conv_w, conv_b).astype(jnp.float32)
    h_rnn = (rg_lru_dir(xc, w_rgate[0], b_rgate[0], w_igate[0], b_igate[0], lru_lambda[0], False)
             + rg_lru_dir(xc, w_rgate[1], b_rgate[1], w_igate[1], b_igate[1], lru_lambda[1], True))
    y_a = jnp.dot(jax.nn.gelu(u_gate) * h_rnn.astype(x.dtype), w_rnn_out)

    lam_init = 0.8 - 0.6 * math.exp(-0.3 * layer_idx)
    lam = (jnp.exp(jnp.sum(lambda_q1.astype(jnp.float32) * lambda_k1.astype(jnp.float32)))
           - jnp.exp(jnp.sum(lambda_q2.astype(jnp.float32) * lambda_k2.astype(jnp.float32)))
           + lam_init)
    y_b = jnp.dot(diff_attention(q, k, v, lam, lam_init, subln_g, rel_bias), w_attn_out)

    mix = jnp.dot(g_merge[..., :D] * y_a + g_merge[..., D:] * y_b, w_o)
    x = x + g_a * mix

    h = rms_norm(x, norm_ffn_g) * (1.0 + sc_f) + sh_f
    hf = h.reshape(B * L, D)
    idx, wts = route(hf, w_router, router_bias)
    y = routed_experts(hf, idx, wts, w1_e, w3_e, w2_e) + swiglu(hf, w1_s, w3_s, w2_s)
    return x + g_f * y.reshape(B, L, D)


def trunk(x, c, norm_mix_g, norm_ffn_g, final_norm_g, w_ada, b_ada, w_in, conv_w, conv_b,
          w_rgate, b_rgate, w_igate, b_igate, lru_lambda, w_rnn_out,
          lambda_q1, lambda_k1, lambda_q2, lambda_k2, subln_g, rel_bias, w_attn_out, w_o,
          w_router, router_bias, w1_e, w3_e, w2_e, w1_s, w3_s, w2_s):
    for l in range(DEPTH):
        x = encoder_layer(x, c, l, norm_mix_g[l], norm_ffn_g[l], w_ada[l], b_ada[l], w_in[l],
                          conv_w[l], conv_b[l], w_rgate[l], b_rgate[l], w_igate[l], b_igate[l],
                          lru_lambda[l], w_rnn_out[l], lambda_q1[l], lambda_k1[l], lambda_q2[l],
                          lambda_k2[l], subln_g[l], rel_bias, w_attn_out[l], w_o[l],
                          w_router[l], router_bias[l], w1_e[l], w3_e[l], w2_e[l],
                          w1_s[l], w3_s[l], w2_s[l])
    return rms_norm(x, final_norm_g)


def setup_inputs(seed: int = 0) -> dict:
    key = jax.random.key(seed)
    ks = jax.random.split(key, 40)
    f32 = jnp.float32

    def nrm(k, shape, scale):
        return jax.random.normal(k, shape, f32) * scale

    D = D_MODEL
    u = jax.random.uniform(ks[14], (DEPTH, 2, D_RNN), f32, 0.9, 0.999)
    p = u ** (1.0 / LRU_C)
    lru_lambda = jnp.log(p) - jnp.log1p(-p)
    return {
        "x_prompt": nrm(ks[0], (BATCH, SEQ, D), 1.0),
        "x_sample": nrm(ks[1], (DEC_BATCH, DEC_SEQ, D), 1.0),
        "c_prompt": nrm(ks[2], (BATCH, D), 1.0),
        "c_sample": nrm(ks[3], (DEC_BATCH, D), 1.0),
        "norm_mix_g": 1.0 + nrm(ks[4], (DEPTH, D), 0.05),
        "norm_ffn_g": 1.0 + nrm(ks[5], (DEPTH, D), 0.05),
        "final_norm_g": 1.0 + nrm(ks[6], (D,), 0.05),
        "w_ada": nrm(ks[7], (DEPTH, D, 6 * D), 0.5 * D ** -0.5),
        "b_ada": nrm(ks[8], (DEPTH, 6 * D), 0.05),
        "w_in": nrm(ks[9], (DEPTH, D, IN_COLS), D ** -0.5),
        "conv_w": nrm(ks[10], (DEPTH, CONV_WIDTH, D_RNN), CONV_WIDTH ** -0.5),
        "conv_b": nrm(ks[11], (DEPTH, D_RNN), 0.02),
        "w_rgate": nrm(ks[12], (DEPTH, 2, LRU_BLOCKS, LRU_BLOCK, LRU_BLOCK), LRU_BLOCK ** -0.5),
        "b_rgate": nrm(ks[13], (DEPTH, 2, D_RNN), 0.1),
        "w_igate": nrm(ks[15], (DEPTH, 2, LRU_BLOCKS, LRU_BLOCK, LRU_BLOCK), LRU_BLOCK ** -0.5),
        "b_igate": nrm(ks[16], (DEPTH, 2, D_RNN), 0.1),
        "lru_lambda": lru_lambda,
        "w_rnn_out": nrm(ks[17], (DEPTH, D_RNN, D), D_RNN ** -0.5),
        "lambda_q1": nrm(ks[18], (DEPTH, HEAD_DIM), 0.1),
        "lambda_k1": nrm(ks[19], (DEPTH, HEAD_DIM), 0.1),
        "lambda_q2": nrm(ks[20], (DEPTH, HEAD_DIM), 0.1),
        "lambda_k2": nrm(ks[21], (DEPTH, HEAD_DIM), 0.1),
        "subln_g": 1.0 + nrm(ks[22], (DEPTH, V_DIM), 0.05),
        "rel_bias": nrm(ks[23], (N_BUCKETS, N_HEADS), 0.5),
        "w_attn_out": nrm(ks[24], (DEPTH, D_ATTN, D), D_ATTN ** -0.5),
        "w_o": nrm(ks[25], (DEPTH, D, D), D ** -0.5),
        "w_router": nrm(ks[26], (DEPTH, D, N_EXPERTS), D ** -0.5),
        "router_bias": nrm(ks[27], (DEPTH, N_EXPERTS), 0.01),
        "w1_e": nrm(ks[28], (DEPTH, N_EXPERTS, D, D_EXPERT), D ** -0.5),
        "w3_e": nrm(ks[29], (DEPTH, N_EXPERTS, D, D_EXPERT), D ** -0.5),
        "w2_e": nrm(ks[30], (DEPTH, N_EXPERTS, D_EXPERT, D), D_EXPERT ** -0.5),
        "w1_s": nrm(ks[31], (DEPTH, D, D_SHARED), D ** -0.5),
        "w3_s": nrm(ks[32], (DEPTH, D, D_SHARED), D ** -0.5),
        "w2_s": nrm(ks[33], (DEPTH, D_SHARED, D), D_SHARED ** -0.5),
    }


def reference(x_prompt, x_sample, c_prompt, c_sample, norm_mix_g, norm_ffn_g, final_norm_g,
              w_ada, b_ada, w_in, conv_w, conv_b, w_rgate, b_rgate, w_igate, b_igate,
              lru_lambda, w_rnn_out, lambda_q1, lambda_k1, lambda_q2, lambda_k2, subln_g,
              rel_bias, w_attn_out, w_o, w_router, router_bias, w1_e, w3_e, w2_e,
              w1_s, w3_s, w2_s):
    weights = (norm_mix_g, norm_ffn_g, final_norm_g, w_ada, b_ada, w_in, conv_w, conv_b,
               w_rgate, b_rgate, w_igate, b_igate, lru_lambda, w_rnn_out,
               lambda_q1, lambda_k1, lambda_q2, lambda_k2, subln_g, rel_bias, w_attn_out, w_o,
               w_router, router_bias, w1_e, w3_e, w2_e, w1_s, w3_s, w2_s)
    y_prompt = trunk(x_prompt, c_prompt, *weights)
    y_sample = trunk(x_sample, c_sample, *weights)
    return (y_prompt, y_sample)
```

```python
import functools
import math

import numpy as np
import jax
import jax.numpy as jnp
from jax import lax
from jax.experimental import pallas as pl
from jax.experimental.pallas import tpu as pltpu

D_MODEL = 1024
D_RNN = 1024
LRU_BLOCK = 64
LRU_C = 8.0
N_HEADS = 8
HEAD_DIM = 64
V_DIM = 2 * HEAD_DIM
N_BUCKETS = 32
MAX_DISTANCE = 128
N_EXPERTS = 64
TOP_K = 8
N_GROUPS = 8
GROUP_SIZE = N_EXPERTS // N_GROUPS
TOPK_GROUPS = 4
D_EXPERT = 256
ROUTED_SCALE = 2.5
EPS = 1e-6
LAM_INIT = 0.8 - 0.6 * math.exp(-0.3 * 0)

VMEM_LIMIT_BYTES = 56 * 1024 * 1024
LANES = 128
SUBLANES = 8
GATE_CHUNK = 256
WEIGHT_COLS = 128

F32 = jnp.float32
BF16 = jnp.bfloat16


def _cparams(*sem):
    return pltpu.CompilerParams(dimension_semantics=sem, vmem_limit_bytes=VMEM_LIMIT_BYTES)


def _dot(a, b):
    return jnp.dot(a, b, preferred_element_type=F32)


def _dot_nt(a, b):
    return lax.dot_general(a, b, (((1,), (1,)), ((), ())), preferred_element_type=F32)


def _split_bf16(x):
    hi = x.astype(BF16)
    lo = (x - hi.astype(F32)).astype(BF16)
    return hi, lo


def _sigmoid(x):
    return 1.0 / (1.0 + jnp.exp(-x))


def _rms(x, g):
    return x * lax.rsqrt(jnp.mean(x * x, axis=-1, keepdims=True) + EPS) * g


def _ada_kernel(c_ref, w_ref, b_ref, o_ref):
    c = c_ref[...]
    sc = c * _sigmoid(c)
    c_hi, c_lo = _split_bf16(sc)
    w_hi, w_lo = _split_bf16(w_ref[...])
    o_ref[...] = _dot(c_hi, w_hi) + _dot(c_lo, w_hi) + _dot(c_hi, w_lo) + b_ref[...]


def _ada(c_all, w_ada, b_ada, tn=1536):
    rows, d = c_all.shape
    n = w_ada.shape[1]
    return pl.pallas_call(
        _ada_kernel,
        out_shape=jax.ShapeDtypeStruct((rows, n), F32),
        grid=(n // tn,),
        in_specs=[pl.BlockSpec((rows, d), lambda j: (0, 0)),
                  pl.BlockSpec((d, tn), lambda j: (0, j)),
                  pl.BlockSpec((1, tn), lambda j: (0, j))],
        out_specs=pl.BlockSpec((rows, tn), lambda j: (0, j)),
        compiler_params=_cparams("parallel"),
        name="ada_mod",
    )(c_all, w_ada, b_ada)


def _inproj_kernel(x_ref, mod_ref, g_ref, w_ref, urnn_ref, ugate_ref, q_ref, k_ref, v_ref,
                   ga_ref, gb_ref):
    d = D_MODEL
    x = x_ref[...]
    h = _rms(x, g_ref[...]) * (1.0 + mod_ref[1:2, :]) + mod_ref[0:1, :]
    hb = h.astype(BF16)

    def col(c):
        return _dot(hb, w_ref[:, c * d:(c + 1) * d])

    urnn_ref[...] = col(0)
    ugate_ref[...] = col(1)
    q = (col(2) * (HEAD_DIM ** -0.5)).astype(BF16)
    k = col(3).astype(BF16)
    v = col(4).astype(BF16)
    for hd in range(N_HEADS):
        sl = slice(hd * V_DIM, (hd + 1) * V_DIM)
        q_ref[hd] = q[:, sl]
        k_ref[hd] = k[:, sl]
        v_ref[hd] = v[:, sl]
    ga_ref[...] = _sigmoid(col(5))
    gb_ref[...] = _sigmoid(col(6))


def _inproj(x, mod, g, w_in_bf, tm):
    b, l, d = x.shape
    ncol = w_in_bf.shape[1]
    row = pl.BlockSpec((None, tm, d), lambda bi, i: (bi, i, 0))
    head = pl.BlockSpec((None, N_HEADS, tm, V_DIM), lambda bi, i: (bi, 0, i, 0))
    f32_out = jax.ShapeDtypeStruct((b, l, d), F32)
    head_out = jax.ShapeDtypeStruct((b, N_HEADS, l, V_DIM), BF16)
    return pl.pallas_call(
        _inproj_kernel,
        out_shape=(f32_out, f32_out, head_out, head_out, head_out, f32_out, f32_out),
        grid=(b, l // tm),
        in_specs=[row,
                  pl.BlockSpec((None, 6, d), lambda bi, i: (bi, 0, 0)),
                  pl.BlockSpec((1, d), lambda bi, i: (0, 0)),
                  pl.BlockSpec((d, ncol), lambda bi, i: (0, 0), pipeline_mode=pl.Buffered(1))],
        out_specs=(row, row, head, head, head, row, row),
        compiler_params=_cparams("parallel", "parallel"),
        name="in_proj",
    )(x, mod, g, w_in_bf)


def _gelu_tanh(x):
    return x * (0.5 * (1.0 + jnp.tanh(math.sqrt(2.0 / math.pi) * (x + 0.044715 * (x * x * x)))))


def _scan_kernel(*refs, tl, nt, reverse):
    if reverse:
        (u_ref, prev_ref, next_ref, cw_ref, cb_ref, wbd_ref, br_ref, bi_ref, lam_ref,
         hf_ref, ug_ref, o_ref, ext_ref, carry_ref) = refs
    else:
        (u_ref, prev_ref, next_ref, cw_ref, cb_ref, wbd_ref, br_ref, bi_ref, lam_ref,
         o_ref, ext_ref, carry_ref) = refs
    c = D_RNN
    step = pl.program_id(1)
    t = (nt - 1 - step) if reverse else step

    @pl.when(step == 0)
    def _():
        carry_ref[...] = jnp.zeros_like(carry_ref)

    ext_ref[0:SUBLANES, :] = jnp.where(t > 0, prev_ref[...], 0.0)
    ext_ref[SUBLANES:SUBLANES + tl, :] = u_ref[...]
    ext_ref[SUBLANES + tl:, :] = jnp.where(t < nt - 1, next_ref[...], 0.0)
    xc = cb_ref[...]
    for j in range(4):
        xc = xc + cw_ref[j:j + 1, :] * ext_ref[SUBLANES - 2 + j:SUBLANES - 2 + j + tl, :]

    xcb = xc.astype(BF16)
    r_parts, i_parts = [], []
    for ch in range(c // GATE_CHUNK):
        z = _dot(xcb[:, ch * GATE_CHUNK:(ch + 1) * GATE_CHUNK], wbd_ref[ch])
        r_parts.append(z[:, :GATE_CHUNK])
        i_parts.append(z[:, GATE_CHUNK:])
    r = _sigmoid(jnp.concatenate(r_parts, axis=-1) + br_ref[...])
    ig = _sigmoid(jnp.concatenate(i_parts, axis=-1) + bi_ref[...])
    nl = -lam_ref[...]
    softplus = jnp.maximum(nl, 0.0) + jnp.log1p(jnp.exp(-jnp.abs(nl)))
    log_a = (-LRU_C) * r * softplus
    a = jnp.exp(log_a)
    bb = jnp.sqrt(-jnp.tanh(log_a) * (1.0 + a * a)) * (ig * xc)

    row = lax.broadcasted_iota(jnp.int32, (tl, c), 0)
    s = 1
    while s < tl:
        if reverse:
            valid = row < tl - s
            shift = tl - s
        else:
            valid = row >= s
            shift = s
        a_n = jnp.where(valid, pltpu.roll(a, shift, 0), 1.0)
        b_n = jnp.where(valid, pltpu.roll(bb, shift, 0), 0.0)
        bb = a * b_n + bb
        a = a * a_n
        s *= 2
    h = a * carry_ref[...] + bb
    carry_ref[...] = h[0:1, :] if reverse else h[tl - 1:tl, :]

    if reverse:
        o_ref[...] = (_gelu_tanh(ug_ref[...]) * (hf_ref[...] + h)).astype(o_ref.dtype)
    else:
        o_ref[...] = h


def _scan(u, conv_w, conv_b, wbd, b_r, b_i, lam, tl, h_fwd=None, u_gate=None):
    b, l, c = u.shape
    nt = l // tl
    reverse = h_fwd is not None
    tpb = tl // SUBLANES
    nblk = l // SUBLANES

    def tix(s):
        return (nt - 1 - s) if reverse else s

    row = pl.BlockSpec((None, tl, c), lambda bi, s: (bi, tix(s), 0))
    prev = pl.BlockSpec((None, SUBLANES, c), lambda bi, s: (bi, jnp.maximum(tix(s) * tpb - 1, 0), 0))
    nxt = pl.BlockSpec((None, SUBLANES, c),
                       lambda bi, s: (bi, jnp.minimum((tix(s) + 1) * tpb, nblk - 1), 0))
    vec = pl.BlockSpec((1, c), lambda bi, s: (0, 0))
    in_specs = [row, prev, nxt,
                pl.BlockSpec((4, c), lambda bi, s: (0, 0)), vec,
                pl.BlockSpec(wbd.shape, lambda bi, s: (0, 0, 0)), vec, vec, vec]
    args = [u, u, u, conv_w, conv_b, wbd, b_r, b_i, lam]
    if reverse:
        in_specs += [row, row]
        args += [h_fwd, u_gate]
        out_dtype = BF16
    else:
        out_dtype = F32
    return pl.pallas_call(
        functools.partial(_scan_kernel, tl=tl, nt=nt, reverse=reverse),
        out_shape=jax.ShapeDtypeStruct((b, l, c), out_dtype),
        grid=(b, nt),
        in_specs=in_specs,
        out_specs=row,
        scratch_shapes=[pltpu.VMEM((tl + 2 * SUBLANES, c), F32), pltpu.VMEM((1, c), F32)],
        compiler_params=_cparams("parallel", "arbitrary"),
        name="rglru_bwd" if reverse else "rglru_fwd",
    )(*args)


def _attn_kernel(far_ref, q_ref, k_ref, v_ref, bias_ref, lamp_ref, g_ref, o_ref,
                 q1_ref, q2_ref, m1_ref, l1_ref, a1_ref, m2_ref, l2_ref, a2_ref, *, nk, rc):
    hd = pl.program_id(1)
    i = pl.program_id(2)
    j = pl.program_id(3)
    t = q_ref.shape[0]
    branches = ((q1_ref, m1_ref, l1_ref, a1_ref), (q2_ref, m2_ref, l2_ref, a2_ref))

    @pl.when(j == 0)
    def _():
        q = q_ref[...]
        lane = lax.broadcasted_iota(jnp.int32, q.shape, 1)
        q1_ref[...] = jnp.where(lane < HEAD_DIM, q, jnp.zeros_like(q))
        q2_ref[...] = jnp.where(lane >= HEAD_DIM, q, jnp.zeros_like(q))
        for _, m_ref, l_ref, a_ref in branches:
            m_ref[...] = jnp.full_like(m_ref, -jnp.inf)
            l_ref[...] = jnp.zeros_like(l_ref)
            a_ref[...] = jnp.zeros_like(a_ref)

    @pl.when(j == jnp.maximum(i - 1, 0))
    def _():
        m1_ref[...] += far_ref[0, hd]
        m2_ref[...] += far_ref[0, hd]

    def step(with_bias):
        k = k_ref[...]
        v = v_ref[...]

        def chunk(ci, carry):
            r0 = pl.multiple_of(ci * rc, rc)
            rows = pl.ds(r0, rc)
            for qz_ref, m_ref, l_ref, a_ref in branches:
                s = _dot_nt(qz_ref[rows, :], k)
                if with_bias:
                    s = s + bias_ref[rows, :]
                m_old = m_ref[rows, :]
                m_new = jnp.maximum(m_old, jnp.max(s, axis=-1, keepdims=True))
                alpha = jnp.exp(m_old - m_new)
                p = jnp.exp(s - m_new)
                l_ref[rows, :] = alpha * l_ref[rows, :] + jnp.sum(p, axis=-1, keepdims=True)
                a_ref[rows, :] = alpha * a_ref[rows, :] + _dot(p.astype(BF16), v)
                m_ref[rows, :] = m_new
            return carry

        lax.fori_loop(0, t // rc, chunk, 0)

    is_band = jnp.abs(j - i) <= 1

    @pl.when(is_band)
    def _():
        step(True)

    @pl.when(jnp.logical_not(is_band))
    def _():
        step(False)

    @pl.when(j == jnp.minimum(i + 1, nk - 1))
    def _():
        m1_ref[...] -= far_ref[1, hd]
        m2_ref[...] -= far_ref[1, hd]

    @pl.when(j == nk - 1)
    def _():
        lp = lamp_ref[...]
        lam = (jnp.exp(jnp.sum(lp[0:1, :] * lp[1:2, :], axis=-1, keepdims=True))
               - jnp.exp(jnp.sum(lp[2:3, :] * lp[3:4, :], axis=-1, keepdims=True)) + LAM_INIT)
        o = a1_ref[...] / l1_ref[...] - lam * (a2_ref[...] / l2_ref[...])
        o = _rms(o, g_ref[...]) * (1.0 - LAM_INIT)
        o_ref[...] = o.astype(o_ref.dtype)


def _attention(q, k, v, bias_tiles, far_bias, lam_params, subln_g, t, rc):
    b, nh, l, dv = q.shape
    nq = l // t
    qspec = pl.BlockSpec((None, None, t, dv), lambda bi, h, i, j: (bi, h, i, 0))
    kspec = pl.BlockSpec((None, None, t, dv), lambda bi, h, i, j: (bi, h, j, 0))
    return pl.pallas_call(
        functools.partial(_attn_kernel, nk=nq, rc=rc),
        out_shape=jax.ShapeDtypeStruct((b, nh, l, dv), BF16),
        grid=(b, nh, nq, nq),
        in_specs=[pl.BlockSpec(memory_space=pltpu.SMEM),
                  qspec, kspec, kspec,
                  pl.BlockSpec((None, None, t, t),
                               lambda bi, h, i, j: (h, jnp.clip(j - i, -1, 1) + 1, 0, 0)),
                  pl.BlockSpec((4, HEAD_DIM), lambda bi, h, i, j: (0, 0)),
                  pl.BlockSpec((1, dv), lambda bi, h, i, j: (0, 0))],
        out_specs=qspec,
        scratch_shapes=[pltpu.VMEM((t, dv), BF16), pltpu.VMEM((t, dv), BF16),
                        pltpu.VMEM((t, 1), F32), pltpu.VMEM((t, 1), F32), pltpu.VMEM((t, dv), F32),
                        pltpu.VMEM((t, 1), F32), pltpu.VMEM((t, 1), F32), pltpu.VMEM((t, dv), F32)],
        compiler_params=_cparams("parallel", "parallel", "parallel", "arbitrary"),
        name="diff_attn",
    )(far_bias, q, k, v, bias_tiles, lam_params, subln_g)


def _merge_kernel(ga_ref, o_ref, gma_ref, gmb_ref, x_ref, mod_ref, g_ref, wrnn_ref, wattn_ref,
                  wo_ref, wrh_ref, wrl_ref, x1_ref, h2_ref, s_ref):
    ya = _dot(ga_ref[...], wrnn_ref[...])
    ob = jnp.concatenate([o_ref[hd] for hd in range(N_HEADS)], axis=-1)
    yb = _dot(ob, wattn_ref[...])
    mixed = (gma_ref[...] * ya + gmb_ref[...] * yb).astype(BF16)
    x1 = x_ref[...] + mod_ref[2:3, :] * _dot(mixed, wo_ref[...])
    x1_ref[...] = x1
    h2 = _rms(x1, g_ref[...]) * (1.0 + mod_ref[4:5, :]) + mod_ref[3:4, :]
    h2_hi, h2_lo = _split_bf16(h2)
    h2_ref[...] = h2_hi
    logits = (_dot_nt(wrh_ref[...], h2_hi) + _dot_nt(wrh_ref[...], h2_lo)
              + _dot_nt(wrl_ref[...], h2_hi))
    s_ref[...] = _sigmoid(logits)


def _merge(ga, o, gm_a, gm_b, x, mod, g, w_rnn_bf, w_attn_bf, w_o_bf, wr_hi, wr_lo, tm):
    b, l, d = x.shape
    row = pl.BlockSpec((None, tm, d), lambda bi, i: (bi, i, 0))
    wspec = pl.BlockSpec((d, d), lambda bi, i: (0, 0), pipeline_mode=pl.Buffered(1))
    rspec = pl.BlockSpec((N_EXPERTS, d), lambda bi, i: (0, 0))
    return pl.pallas_call(
        _merge_kernel,
        out_shape=(jax.ShapeDtypeStruct((b, l, d), F32),
                   jax.ShapeDtypeStruct((b, l, d), BF16),
                   jax.ShapeDtypeStruct((b, N_EXPERTS, l), F32)),
        grid=(b, l // tm),
        in_specs=[row,
                  pl.BlockSpec((None, N_HEADS, tm, V_DIM), lambda bi, i: (bi, 0, i, 0)),
                  row, row, row,
                  pl.BlockSpec((None, 6, d), lambda bi, i: (bi, 0, 0)),
                  pl.BlockSpec((1, d), lambda bi, i: (0, 0)),
                  wspec, wspec, wspec, rspec, rspec],
        out_specs=(row, row, pl.BlockSpec((None, N_EXPERTS, tm), lambda bi, i: (bi, 0, i))),
        compiler_params=_cparams("parallel", "parallel"),
        name="merge_router",
    )(ga, o, gm_a, gm_b, x, mod, g, w_rnn_bf, w_attn_bf, w_o_bf, wr_hi, wr_lo)


def _route_kernel(s_ref, rb_ref, w_ref):
    tr = s_ref.shape[1]
    neg = -jnp.inf
    sub = lax.broadcasted_iota(jnp.int32, (GROUP_SIZE, tr), 0)
    s_g = [s_ref[g * GROUP_SIZE:(g + 1) * GROUP_SIZE, :] for g in range(N_GROUPS)]
    sb_g = [s_g[g] + rb_ref[g * GROUP_SIZE:(g + 1) * GROUP_SIZE, :] for g in range(N_GROUPS)]

    gscore = []
    for g in range(N_GROUPS):
        xg = sb_g[g]
        top1 = jnp.max(xg, axis=0, keepdims=True)
        first = jnp.min(jnp.where(xg == top1, sub, GROUP_SIZE), axis=0, keepdims=True)
        top2 = jnp.max(jnp.where(sub == first, neg, xg), axis=0, keepdims=True)
        gscore.append(top1 + top2)

    masked = []
    for g in range(N_GROUPS):
        rank = jnp.zeros((1, tr), jnp.int32)
        for g2 in range(N_GROUPS):
            if g2 == g:
                continue
            ahead = (gscore[g2] >= gscore[g]) if g2 < g else (gscore[g2] > gscore[g])
            rank = rank + ahead.astype(jnp.int32)
        keep = jnp.broadcast_to(rank < TOPK_GROUPS, (GROUP_SIZE, tr))
        masked.append(jnp.where(keep, sb_g[g], neg))

    ranks = [jnp.zeros((GROUP_SIZE, tr), jnp.int32) for _ in range(N_GROUPS)]
    for g2 in range(N_GROUPS):
        for r2 in range(GROUP_SIZE):
            other = jnp.broadcast_to(masked[g2][r2:r2 + 1, :], (GROUP_SIZE, tr))
            for g in range(N_GROUPS):
                mine = masked[g]
                if g2 < g:
                    ahead = other >= mine
                elif g2 > g:
                    ahead = other > mine
                else:
                    ahead = (other > mine) | ((other == mine) & (sub > r2))
                ranks[g] = ranks[g] + ahead.astype(jnp.int32)

    picked = [jnp.where(ranks[g] < TOP_K, s_g[g], 0.0) for g in range(N_GROUPS)]
    total = picked[0]
    for g in range(1, N_GROUPS):
        total = total + picked[g]
    denom = jnp.sum(total, axis=0, keepdims=True)
    for g in range(N_GROUPS):
        w_ref[g * GROUP_SIZE:(g + 1) * GROUP_SIZE, :] = picked[g] / denom * ROUTED_SCALE
    w_ref[N_EXPERTS:N_EXPERTS + GROUP_SIZE, :] = jnp.where(sub == 0, 1.0, 0.0)
    w_ref[N_EXPERTS + GROUP_SIZE:, :] = jnp.zeros((WEIGHT_COLS - N_EXPERTS - GROUP_SIZE, tr), F32)


def _route(s_t, router_bias_col, tr):
    b, ne, l = s_t.shape
    return pl.pallas_call(
        _route_kernel,
        out_shape=jax.ShapeDtypeStruct((b, WEIGHT_COLS, l), F32),
        grid=(b, l // tr),
        in_specs=[pl.BlockSpec((None, ne, tr), lambda bi, i: (bi, 0, i)),
                  pl.BlockSpec((ne, 1), lambda bi, i: (0, 0))],
        out_specs=pl.BlockSpec((None, WEIGHT_COLS, tr), lambda bi, i: (bi, 0, i)),
        compiler_params=_cparams("parallel", "parallel"),
        name="route_topk",
    )(s_t, router_bias_col)


def _moe_kernel(h_ref, w_ref, x1_ref, mod_ref, g_ref, w1_ref, w3_ref, w2_ref, o_ref, acc_ref, *, ne):
    e = pl.program_id(2)

    @pl.when(e == 0)
    def _():
        acc_ref[...] = jnp.zeros_like(acc_ref)

    h = h_ref[...]
    a1 = _dot(h, w1_ref[...])
    a3 = _dot(h, w3_ref[...])
    w_hi, w_lo = _split_bf16(w_ref[...])
    pick = (lax.broadcasted_iota(jnp.int32, (2 * WEIGHT_COLS, D_EXPERT), 0) % WEIGHT_COLS == e)
    wcol = _dot(jnp.concatenate([w_hi, w_lo], axis=-1), pick.astype(BF16))
    hidden = (a1 * _sigmoid(a1)) * a3 * wcol
    acc_ref[...] += _dot(hidden.astype(BF16), w2_ref[...])

    @pl.when(e == ne - 1)
    def _():
        y = x1_ref[...] + mod_ref[5:6, :] * acc_ref[...]
        o_ref[...] = _rms(y, g_ref[...])


def _moe(h2, w_tok, x1, mod, final_g, w1_all, w3_all, w2_all, tm):
    b, l, d = x1.shape
    ne = w1_all.shape[0]
    row = pl.BlockSpec((None, tm, d), lambda bi, i, e: (bi, i, 0))
    return pl.pallas_call(
        functools.partial(_moe_kernel, ne=ne),
        out_shape=jax.ShapeDtypeStruct((b, l, d), F32),
        grid=(b, l // tm, ne),
        in_specs=[row,
                  pl.BlockSpec((None, tm, WEIGHT_COLS), lambda bi, i, e: (bi, i, 0)),
                  row,
                  pl.BlockSpec((None, 6, d), lambda bi, i, e: (bi, 0, 0)),
                  pl.BlockSpec((1, d), lambda bi, i, e: (0, 0)),
                  pl.BlockSpec((None, d, D_EXPERT), lambda bi, i, e: (e, 0, 0)),
                  pl.BlockSpec((None, d, D_EXPERT), lambda bi, i, e: (e, 0, 0)),
                  pl.BlockSpec((None, D_EXPERT, d), lambda bi, i, e: (e, 0, 0))],
        out_specs=row,
        scratch_shapes=[pltpu.VMEM((tm, d), F32)],
        compiler_params=_cparams("parallel", "parallel", "arbitrary"),
        name="moe_experts",
    )(h2, w_tok, x1, mod, final_g, w1_all, w3_all, w2_all)


def _t5_bucket_np(rel):
    half = N_BUCKETS // 2
    max_exact = half // 2
    ret = np.where(rel > 0, half, 0)
    n = np.abs(rel)
    nf = np.maximum(n, 1).astype(np.float32)
    large = max_exact + (np.log(nf / np.float32(max_exact)) / np.float32(math.log(MAX_DISTANCE / max_exact))
                         * np.float32(half - max_exact)).astype(np.int32)
    large = np.minimum(large, half - 1)
    return (ret + np.where(n < max_exact, n, large)).astype(np.int32)


def _bias_tiles(rel_bias, t):
    assert t >= MAX_DISTANCE
    rel = np.arange(-2 * t, 2 * t + 1)
    table = jnp.take(rel_bias.astype(F32), jnp.asarray(_t5_bucket_np(rel)), axis=0).T
    tiles = []
    for off in (-t, 0, t):
        start = off - (t - 1) + 2 * t
        w = lax.slice_in_dim(table, start, start + 2 * t, axis=1)
        skew = jnp.tile(w, (1, t))[:, :t * (2 * t - 1)].reshape(-1, t, 2 * t - 1)
        tiles.append(skew[:, :, t - 1:])
    far = jnp.stack([rel_bias[int(_t5_bucket_np(np.array(-MAX_DISTANCE)))],
                     rel_bias[int(_t5_bucket_np(np.array(MAX_DISTANCE)))]]).astype(F32)
    return jnp.stack(tiles, axis=1), far


def _block_diag_gates(w_r, w_i):
    per = GATE_CHUNK // LRU_BLOCK
    eye = jnp.eye(per, dtype=F32)

    def bd(w):
        w = w.reshape(-1, per, LRU_BLOCK, LRU_BLOCK)
        return jnp.einsum('cjab,jk->cjakb', w, eye).reshape(-1, GATE_CHUNK, GATE_CHUNK)

    return jnp.concatenate([bd(w_r), bd(w_i)], axis=-1).astype(BF16)


def _tile(l, pref):
    return min(l, pref)


def _trunk(x, mod, p, tiles=None):
    b, l, d = x.shape
    tl = dict(inproj=512, scan=256, attn=1024, attn_rows=256, merge=512, route=512, moe=1024)
    if tiles:
        tl.update(tiles)
    tl = {k: _tile(l, v) for k, v in tl.items()}
    tl['attn_rows'] = min(tl['attn_rows'], tl['attn'])

    u_rnn, u_gate, q, k, v, gm_a, gm_b = _inproj(x, mod, p['norm_mix_g'], p['w_in'], tl['inproj'])

    h_fwd = _scan(u_rnn, p['conv_w'], p['conv_b'], p['wbd'][0], p['b_r'][0:1], p['b_i'][0:1],
                  p['lam'][0:1], tl['scan'])
    ga = _scan(u_rnn, p['conv_w'], p['conv_b'], p['wbd'][1], p['b_r'][1:2], p['b_i'][1:2],
               p['lam'][1:2], tl['scan'], h_fwd=h_fwd, u_gate=u_gate)

    bias_tiles, far = p['bias'](tl['attn'])
    o = _attention(q, k, v, bias_tiles, far, p['lam_params'], p['subln_g'], tl['attn'], tl['attn_rows'])

    x1, h2, s_t = _merge(ga, o, gm_a, gm_b, x, mod, p['norm_ffn_g'], p['w_rnn_out'], p['w_attn_out'],
                         p['w_o'], p['wr_hi'], p['wr_lo'], tl['merge'])
    w_t = _route(s_t, p['router_bias'], tl['route'])
    w_tok = jnp.swapaxes(w_t, 1, 2)
    return _moe(h2, w_tok, x1, mod, p['final_norm_g'], p['w1'], p['w3'], p['w2'], tl['moe'])


def _prepare(norm_mix_g, norm_ffn_g, final_norm_g, w_in, conv_w, conv_b, w_rgate, b_rgate, w_igate,
             b_igate, lru_lambda, w_rnn_out, lambda_q1, lambda_k1, lambda_q2, lambda_k2, subln_g,
             rel_bias, w_attn_out, w_o, w_router, router_bias, w1_e, w3_e, w2_e, w1_s, w3_s, w2_s):
    wr_t = w_router[0].T.astype(F32)
    wr_hi = wr_t.astype(BF16)
    wr_lo = (wr_t - wr_hi.astype(F32)).astype(BF16)
    bias_cache = {}

    def bias(t):
        if t not in bias_cache:
            bias_cache[t] = _bias_tiles(rel_bias, t)
        return bias_cache[t]

    return dict(
        norm_mix_g=norm_mix_g[0][None], norm_ffn_g=norm_ffn_g[0][None], final_norm_g=final_norm_g[None],
        w_in=w_in[0].astype(BF16), conv_w=conv_w[0], conv_b=conv_b[0][None],
        wbd=jnp.stack([_block_diag_gates(w_rgate[0, dr], w_igate[0, dr]) for dr in range(2)]),
        b_r=b_rgate[0], b_i=b_igate[0], lam=lru_lambda[0],
        w_rnn_out=w_rnn_out[0].astype(BF16), w_attn_out=w_attn_out[0].astype(BF16),
        w_o=w_o[0].astype(BF16),
        lam_params=jnp.stack([lambda_q1[0], lambda_k1[0], lambda_q2[0], lambda_k2[0]]).astype(F32),
        subln_g=subln_g[0][None], bias=bias,
        wr_hi=wr_hi, wr_lo=wr_lo, router_bias=router_bias[0][:, None].astype(F32),
        w1=jnp.concatenate([w1_e[0], w1_s[0][None]], axis=0).astype(BF16),
        w3=jnp.concatenate([w3_e[0], w3_s[0][None]], axis=0).astype(BF16),
        w2=jnp.concatenate([w2_e[0], w2_s[0][None]], axis=0).astype(BF16),
    )


def kernel(x_prompt, x_sample, c_prompt, c_sample, norm_mix_g, norm_ffn_g, final_norm_g, w_ada, b_ada, w_in, conv_w, conv_b, w_rgate, b_rgate, w_igate, b_igate, lru_lambda, w_rnn_out, lambda_q1, lambda_k1, lambda_q2, lambda_k2, subln_g, rel_bias, w_attn_out, w_o, w_router, router_bias, w1_e, w3_e, w2_e, w1_s, w3_s, w2_s):
    d = D_MODEL
    nb_p, nb_s = c_prompt.shape[0], c_sample.shape[0]
    pad = (-(nb_p + nb_s)) % SUBLANES
    c_all = jnp.concatenate([c_prompt, c_sample, jnp.zeros((pad, d), F32)], axis=0)
    mod = _ada(c_all, w_ada[0], b_ada[0][None]).reshape(-1, 6, d)
    p = _prepare(norm_mix_g, norm_ffn_g, final_norm_g, w_in, conv_w, conv_b, w_rgate, b_rgate,
                 w_igate, b_igate, lru_lambda, w_rnn_out, lambda_q1, lambda_k1, lambda_q2, lambda_k2,
                 subln_g, rel_bias, w_attn_out, w_o, w_router, router_bias, w1_e, w3_e, w2_e,
                 w1_s, w3_s, w2_s)
    y_prompt = _trunk(x_prompt, mod[:nb_p], p)
    y_sample = _trunk(x_sample, mod[nb_p:nb_p + nb_s], p)
    return (y_prompt, y_sample)
```

```python
import functools
import math

import numpy as np
import jax
import jax.numpy as jnp
from jax import lax
from jax.experimental import pallas as pl
from jax.experimental.pallas import tpu as pltpu

D_MODEL = 1024
D_RNN = 1024
LRU_BLOCK = 64
LRU_C = 8.0
N_HEADS = 8
HEAD_DIM = 64
V_DIM = 2 * HEAD_DIM
N_BUCKETS = 32
MAX_DISTANCE = 128
N_EXPERTS = 64
TOP_K = 8
N_GROUPS = 8
GROUP_SIZE = N_EXPERTS // N_GROUPS
TOPK_GROUPS = 4
D_EXPERT = 256
ROUTED_SCALE = 2.5
EPS = 1e-6
LAM_INIT = 0.8 - 0.6 * math.exp(-0.3 * 0)

VMEM_LIMIT_BYTES = 56 * 1024 * 1024
LANES = 128
SUBLANES = 8
GATE_CHUNK = 256
KEY_BLOCK = 256
PIPE_SKEW = 2
V_ROWS = V_DIM + 16
LOG2E = math.log2(math.e)
WEIGHT_COLS = 128

F32 = jnp.float32
BF16 = jnp.bfloat16


def _cparams(*sem):
    return pltpu.CompilerParams(dimension_semantics=sem, vmem_limit_bytes=VMEM_LIMIT_BYTES)


def _dot(a, b):
    return jnp.dot(a, b, preferred_element_type=F32)


def _dot_nt(a, b):
    return lax.dot_general(a, b, (((1,), (1,)), ((), ())), preferred_element_type=F32)


def _split_bf16(x):
    hi = x.astype(BF16)
    lo = (x - hi.astype(F32)).astype(BF16)
    return hi, lo


def _sigmoid(x):
    return 1.0 / (1.0 + jnp.exp(-x))


def _rms(x, g):
    return x * lax.rsqrt(jnp.mean(x * x, axis=-1, keepdims=True) + EPS) * g


def _ada_kernel(c_ref, w_ref, b_ref, o_ref):
    c = c_ref[...]
    sc = c * _sigmoid(c)
    c_hi, c_lo = _split_bf16(sc)
    w_hi, w_lo = _split_bf16(w_ref[...])
    o_ref[...] = _dot(c_hi, w_hi) + _dot(c_lo, w_hi) + _dot(c_hi, w_lo) + b_ref[...]


def _ada(c_all, w_ada, b_ada, tn=1536):
    rows, d = c_all.shape
    n = w_ada.shape[1]
    return pl.pallas_call(
        _ada_kernel,
        out_shape=jax.ShapeDtypeStruct((rows, n), F32),
        grid=(n // tn,),
        in_specs=[pl.BlockSpec((rows, d), lambda j: (0, 0)),
                  pl.BlockSpec((d, tn), lambda j: (0, j)),
                  pl.BlockSpec((1, tn), lambda j: (0, j))],
        out_specs=pl.BlockSpec((rows, tn), lambda j: (0, j)),
        compiler_params=_cparams("parallel"),
        name="ada_mod",
    )(c_all, w_ada, b_ada)


def _inproj_kernel(x_ref, mod_ref, g_ref, w_ref, wvt_ref, urnn_ref, ugate_ref, q_ref, k_ref, vt_ref,
                   ga_ref, gb_ref):
    d = D_MODEL
    x = x_ref[...]
    h = _rms(x, g_ref[...]) * (1.0 + mod_ref[1:2, :]) + mod_ref[0:1, :]
    hb = h.astype(BF16)

    def col(c):
        return _dot(hb, w_ref[:, c * d:(c + 1) * d])

    urnn_ref[...] = col(0)
    ugate_ref[...] = col(1)
    q = (col(2) * (HEAD_DIM ** -0.5 * LOG2E)).astype(BF16)
    k = col(3).astype(BF16)
    vt = _dot_nt(wvt_ref[...], hb).astype(BF16)
    for hd in range(N_HEADS):
        sl = slice(hd * V_DIM, (hd + 1) * V_DIM)
        q_ref[hd] = q[:, sl]
        k_ref[hd] = k[:, sl]
        vt_ref[hd] = vt[sl, :]
    ga_ref[...] = _sigmoid(col(4))
    gb_ref[...] = _sigmoid(col(5))


def _inproj(x, mod, g, w_main_bf, w_vt_bf, tm):
    b, l, d = x.shape
    ncol = w_main_bf.shape[1]
    row = pl.BlockSpec((None, tm, d), lambda bi, i: (bi, i, 0))
    head = pl.BlockSpec((None, N_HEADS, tm, V_DIM), lambda bi, i: (bi, 0, i, 0))
    head_t = pl.BlockSpec((None, N_HEADS, V_DIM, tm), lambda bi, i: (bi, 0, 0, i))
    f32_out = jax.ShapeDtypeStruct((b, l, d), F32)
    head_out = jax.ShapeDtypeStruct((b, N_HEADS, l, V_DIM), BF16)
    head_t_out = jax.ShapeDtypeStruct((b, N_HEADS, V_DIM, l), BF16)
    return pl.pallas_call(
        _inproj_kernel,
        out_shape=(f32_out, f32_out, head_out, head_out, head_t_out, f32_out, f32_out),
        grid=(b, l // tm),
        in_specs=[row,
                  pl.BlockSpec((None, 6, d), lambda bi, i: (bi, 0, 0)),
                  pl.BlockSpec((1, d), lambda bi, i: (0, 0)),
                  pl.BlockSpec((d, ncol), lambda bi, i: (0, 0), pipeline_mode=pl.Buffered(1)),
                  pl.BlockSpec((d, d), lambda bi, i: (0, 0), pipeline_mode=pl.Buffered(1))],
        out_specs=(row, row, head, head, head_t, row, row),
        compiler_params=_cparams("parallel", "parallel"),
        name="in_proj",
    )(x, mod, g, w_main_bf, w_vt_bf)


def _gelu_tanh(x):
    return x * (0.5 * (1.0 + jnp.tanh(math.sqrt(2.0 / math.pi) * (x + 0.044715 * (x * x * x)))))


def _scan_kernel(*refs, tl, nt, reverse):
    if reverse:
        (u_ref, prev_ref, next_ref, cw_ref, cb_ref, wbd_ref, br_ref, bi_ref, lam_ref,
         hf_ref, ug_ref, o_ref, ext_ref, carry_ref) = refs
    else:
        (u_ref, prev_ref, next_ref, cw_ref, cb_ref, wbd_ref, br_ref, bi_ref, lam_ref,
         o_ref, ext_ref, carry_ref) = refs
    c = D_RNN
    step = pl.program_id(1)
    t = (nt - 1 - step) if reverse else step

    @pl.when(step == 0)
    def _():
        carry_ref[...] = jnp.zeros_like(carry_ref)

    ext_ref[0:SUBLANES, :] = jnp.where(t > 0, prev_ref[...], 0.0)
    ext_ref[SUBLANES:SUBLANES + tl, :] = u_ref[...]
    ext_ref[SUBLANES + tl:, :] = jnp.where(t < nt - 1, next_ref[...], 0.0)
    xc = cb_ref[...]
    for j in range(4):
        xc = xc + cw_ref[j:j + 1, :] * ext_ref[SUBLANES - 2 + j:SUBLANES - 2 + j + tl, :]

    xcb = xc.astype(BF16)
    r_parts, i_parts = [], []
    for ch in range(c // GATE_CHUNK):
        z = _dot(xcb[:, ch * GATE_CHUNK:(ch + 1) * GATE_CHUNK], wbd_ref[ch])
        r_parts.append(z[:, :GATE_CHUNK])
        i_parts.append(z[:, GATE_CHUNK:])
    r = _sigmoid(jnp.concatenate(r_parts, axis=-1) + br_ref[...])
    ig = _sigmoid(jnp.concatenate(i_parts, axis=-1) + bi_ref[...])
    nl = -lam_ref[...]
    softplus = jnp.maximum(nl, 0.0) + jnp.log1p(jnp.exp(-jnp.abs(nl)))
    log_a = (-LRU_C) * r * softplus
    a = jnp.exp(log_a)
    bb = jnp.sqrt(-jnp.tanh(log_a) * (1.0 + a * a)) * (ig * xc)

    row = lax.broadcasted_iota(jnp.int32, (tl, c), 0)
    s = 1
    while s < tl:
        if reverse:
            valid = row < tl - s
            shift = tl - s
        else:
            valid = row >= s
            shift = s
        a_n = jnp.where(valid, pltpu.roll(a, shift, 0), 1.0)
        b_n = jnp.where(valid, pltpu.roll(bb, shift, 0), 0.0)
        bb = a * b_n + bb
        a = a * a_n
        s *= 2
    h = a * carry_ref[...] + bb
    carry_ref[...] = h[0:1, :] if reverse else h[tl - 1:tl, :]

    if reverse:
        o_ref[...] = (_gelu_tanh(ug_ref[...]) * (hf_ref[...] + h)).astype(o_ref.dtype)
    else:
        o_ref[...] = h


def _scan(u, conv_w, conv_b, wbd, b_r, b_i, lam, tl, h_fwd=None, u_gate=None):
    b, l, c = u.shape
    nt = l // tl
    reverse = h_fwd is not None
    tpb = tl // SUBLANES
    nblk = l // SUBLANES

    def tix(s):
        return (nt - 1 - s) if reverse else s

    row = pl.BlockSpec((None, tl, c), lambda bi, s: (bi, tix(s), 0))
    prev = pl.BlockSpec((None, SUBLANES, c), lambda bi, s: (bi, jnp.maximum(tix(s) * tpb - 1, 0), 0))
    nxt = pl.BlockSpec((None, SUBLANES, c),
                       lambda bi, s: (bi, jnp.minimum((tix(s) + 1) * tpb, nblk - 1), 0))
    vec = pl.BlockSpec((1, c), lambda bi, s: (0, 0))
    in_specs = [row, prev, nxt,
                pl.BlockSpec((4, c), lambda bi, s: (0, 0)), vec,
                pl.BlockSpec(wbd.shape, lambda bi, s: (0, 0, 0)), vec, vec, vec]
    args = [u, u, u, conv_w, conv_b, wbd, b_r, b_i, lam]
    if reverse:
        in_specs += [row, row]
        args += [h_fwd, u_gate]
        out_dtype = BF16
    else:
        out_dtype = F32
    return pl.pallas_call(
        functools.partial(_scan_kernel, tl=tl, nt=nt, reverse=reverse),
        out_shape=jax.ShapeDtypeStruct((b, l, c), out_dtype),
        grid=(b, nt),
        in_specs=in_specs,
        out_specs=row,
        scratch_shapes=[pltpu.VMEM((tl + 2 * SUBLANES, c), F32), pltpu.VMEM((1, c), F32)],
        compiler_params=_cparams("parallel", "arbitrary"),
        name="rglru_bwd" if reverse else "rglru_fwd",
    )(*args)


def _bias_kernel(z_ref, o_ref, *, t, rc):
    y = jnp.broadcast_to(z_ref[...], (LANES, 2 * t))
    row = lax.broadcasted_iota(jnp.int32, (LANES, 2 * t), 0)
    bit = 0
    while (1 << bit) < LANES:
        y = jnp.where(((row >> bit) & 1) == 1, pltpu.roll(y, 1 << bit, 1), y)
        bit += 1
    for rh in range(t // LANES):
        blk = pltpu.roll(y, LANES * rh, 1) if rh else y
        for c in range(t // rc):
            o_ref[c, rh * LANES:(rh + 1) * LANES, :] = blk[:, c * rc:(c + 1) * rc]


def _bias_tiles(z, t, rc):
    nh = z.shape[0]
    return pl.pallas_call(
        functools.partial(_bias_kernel, t=t, rc=rc),
        out_shape=jax.ShapeDtypeStruct((nh, 3, t // rc, t, rc), F32),
        grid=(nh, 3),
        in_specs=[pl.BlockSpec((None, None, 1, 2 * t), lambda h, o: (h, o, 0, 0))],
        out_specs=pl.BlockSpec((None, None, t // rc, t, rc), lambda h, o: (h, o, 0, 0, 0)),
        compiler_params=_cparams("parallel", "parallel"),
        name="bias_tiles",
    )(z)


def _max_over_row_groups(s):
    parts = [s[r:r + SUBLANES, :] for r in range(0, s.shape[0], SUBLANES)]
    while len(parts) > 1:
        parts = [jnp.maximum(a, b) for a, b in zip(parts[0::2], parts[1::2])] + \
                ([parts[-1]] if len(parts) % 2 else [])
    return parts[0]


def _attn_kernel(far_ref, q_ref, k_ref, vt_ref, bias_ref, lamp_ref, g_ref, o_ref,
                 qz_ref, vta_ref, m_ref, alpha_ref, acc_ref, s_ref, *, nk, rc):
    hd = pl.program_id(1)
    i = pl.program_id(2)
    j = pl.program_id(3)
    t = q_ref.shape[0]
    nc = t // rc

    @pl.when(j == 0)
    def _():
        q = q_ref[...]
        lane = lax.broadcasted_iota(jnp.int32, q.shape, 1)
        qz_ref[0] = jnp.where(lane < HEAD_DIM, q, jnp.zeros_like(q))
        qz_ref[1] = jnp.where(lane >= HEAD_DIM, q, jnp.zeros_like(q))
        m_ref[...] = jnp.full_like(m_ref, -jnp.inf)
        acc_ref[...] = jnp.zeros_like(acc_ref)
        ones_row = lax.broadcasted_iota(jnp.int32, (V_ROWS - V_DIM, t), 0) == 0
        vta_ref[V_DIM:, :] = jnp.where(ones_row, 1.0, 0.0).astype(BF16)

    vta_ref[0:V_DIM, :] = vt_ref[...]

    @pl.when(j == jnp.maximum(i - 1, 0))
    def _():
        m_ref[...] += far_ref[0, hd]

    def step(with_bias):
        nunits = 2 * nc
        nkb = t // KEY_BLOCK
        for slot in range(nunits + PIPE_SKEW):
            ua = slot if slot < nunits else None
            ub = slot - PIPE_SKEW if slot >= PIPE_SKEW else None
            if ua is not None:
                ca, bra = divmod(ua, 2)
                qa = qz_ref[bra, ca * rc:(ca + 1) * rc, :]
                colmax = None
            if ub is not None:
                cb, brb = divmod(ub, 2)
                m_b = m_ref[brb, cb]
                accp = None
            for kb in range(nkb):
                ks = slice(kb * KEY_BLOCK, (kb + 1) * KEY_BLOCK)
                if ua is not None:
                    s = _dot_nt(k_ref[ks, :], qa)
                    if with_bias:
                        s = s + bias_ref[ca, ks, :]
                    s_ref[ua % (PIPE_SKEW + 1), ks, :] = s
                    bmax = _max_over_row_groups(s)
                    colmax = bmax if colmax is None else jnp.maximum(colmax, bmax)
                if ub is not None:
                    p = jnp.exp2(s_ref[ub % (PIPE_SKEW + 1), ks, :] - m_b)
                    pv = _dot(vta_ref[:, ks], p.astype(BF16))
                    accp = pv if accp is None else accp + pv
            if ua is not None:
                m_old = m_ref[bra, ca]
                m_new = jnp.maximum(m_old, jnp.max(colmax, axis=0, keepdims=True))
                alpha_ref[bra, ca] = jnp.exp2(m_old - m_new)
                m_ref[bra, ca] = m_new
            if ub is not None:
                acc_ref[brb, cb] = alpha_ref[brb, cb] * acc_ref[brb, cb] + accp

    is_band = jnp.abs(j - i) <= 1

    @pl.when(is_band)
    def _():
        step(True)

    @pl.when(jnp.logical_not(is_band))
    def _():
        step(False)

    @pl.when(j == jnp.minimum(i + 1, nk - 1))
    def _():
        m_ref[...] -= far_ref[1, hd]

    @pl.when(j == nk - 1)
    def _():
        lp = lamp_ref[...]
        lam = (jnp.exp(jnp.sum(lp[0:1, :] * lp[1:2, :], axis=-1, keepdims=True))
               - jnp.exp(jnp.sum(lp[2:3, :] * lp[3:4, :], axis=-1, keepdims=True)) + LAM_INIT)
        for c in range(nc):
            o = (acc_ref[0, c, 0:V_DIM, :] / acc_ref[0, c, V_DIM:V_DIM + 1, :]
                 - lam * (acc_ref[1, c, 0:V_DIM, :] / acc_ref[1, c, V_DIM:V_DIM + 1, :]))
            o = o * lax.rsqrt(jnp.mean(o * o, axis=0, keepdims=True) + EPS) * g_ref[...]
            o = o * (1.0 - LAM_INIT)
            o_ref[c * rc:(c + 1) * rc, :] = o.T.astype(o_ref.dtype)


def _attention(q, k, vt, bias_tiles, far_bias, lam_params, subln_g_col, t, rc):
    b, nh, l, dv = q.shape
    nq = l // t
    nc = t // rc
    qspec = pl.BlockSpec((None, None, t, dv), lambda bi, h, i, j: (bi, h, i, 0))
    kspec = pl.BlockSpec((None, None, t, dv), lambda bi, h, i, j: (bi, h, j, 0))
    vspec = pl.BlockSpec((None, None, dv, t), lambda bi, h, i, j: (bi, h, 0, j))
    return pl.pallas_call(
        functools.partial(_attn_kernel, nk=nq, rc=rc),
        out_shape=jax.ShapeDtypeStruct((b, nh, l, dv), BF16),
        grid=(b, nh, nq, nq),
        in_specs=[pl.BlockSpec(memory_space=pltpu.SMEM),
                  qspec, kspec, vspec,
                  pl.BlockSpec((None, None, nc, t, rc),
                               lambda bi, h, i, j: (h, jnp.clip(j - i, -1, 1) + 1, 0, 0, 0)),
                  pl.BlockSpec((4, HEAD_DIM), lambda bi, h, i, j: (0, 0)),
                  pl.BlockSpec((dv, 1), lambda bi, h, i, j: (0, 0))],
        out_specs=qspec,
        scratch_shapes=[pltpu.VMEM((2, t, dv), BF16), pltpu.VMEM((V_ROWS, t), BF16),
                        pltpu.VMEM((2, nc, 1, rc), F32), pltpu.VMEM((2, nc, 1, rc), F32),
                        pltpu.VMEM((2, nc, V_ROWS, rc), F32),
                        pltpu.VMEM((PIPE_SKEW + 1, t, rc), F32)],
        compiler_params=_cparams("parallel", "parallel", "parallel", "arbitrary"),
        name="diff_attn",
    )(far_bias, q, k, vt, bias_tiles, lam_params, subln_g_col)


def _merge_kernel(ga_ref, o_ref, gma_ref, gmb_ref, x_ref, mod_ref, g_ref, wrnn_ref, wattn_ref,
                  wo_ref, wrh_ref, wrl_ref, x1_ref, h2_ref, s_ref):
    ya = _dot(ga_ref[...], wrnn_ref[...])
    ob = jnp.concatenate([o_ref[hd] for hd in range(N_HEADS)], axis=-1)
    yb = _dot(ob, wattn_ref[...])
    mixed = (gma_ref[...] * ya + gmb_ref[...] * yb).astype(BF16)
    x1 = x_ref[...] + mod_ref[2:3, :] * _dot(mixed, wo_ref[...])
    x1_ref[...] = x1
    h2 = _rms(x1, g_ref[...]) * (1.0 + mod_ref[4:5, :]) + mod_ref[3:4, :]
    h2_hi, h2_lo = _split_bf16(h2)
    h2_ref[...] = h2_hi
    logits = (_dot_nt(wrh_ref[...], h2_hi) + _dot_nt(wrh_ref[...], h2_lo)
              + _dot_nt(wrl_ref[...], h2_hi))
    s_ref[...] = _sigmoid(logits)


def _merge(ga, o, gm_a, gm_b, x, mod, g, w_rnn_bf, w_attn_bf, w_o_bf, wr_hi, wr_lo, tm):
    b, l, d = x.shape
    row = pl.BlockSpec((None, tm, d), lambda bi, i: (bi, i, 0))
    wspec = pl.BlockSpec((d, d), lambda bi, i: (0, 0), pipeline_mode=pl.Buffered(1))
    rspec = pl.BlockSpec((N_EXPERTS, d), lambda bi, i: (0, 0))
    return pl.pallas_call(
        _merge_kernel,
        out_shape=(jax.ShapeDtypeStruct((b, l, d), F32),
                   jax.ShapeDtypeStruct((b, l, d), BF16),
                   jax.ShapeDtypeStruct((b, N_EXPERTS, l), F32)),
        grid=(b, l // tm),
        in_specs=[row,
                  pl.BlockSpec((None, N_HEADS, tm, V_DIM), lambda bi, i: (bi, 0, i, 0)),
                  row, row, row,
                  pl.BlockSpec((None, 6, d), lambda bi, i: (bi, 0, 0)),
                  pl.BlockSpec((1, d), lambda bi, i: (0, 0)),
                  wspec, wspec, wspec, rspec, rspec],
        out_specs=(row, row, pl.BlockSpec((None, N_EXPERTS, tm), lambda bi, i: (bi, 0, i))),
        compiler_params=_cparams("parallel", "parallel"),
        name="merge_router",
    )(ga, o, gm_a, gm_b, x, mod, g, w_rnn_bf, w_attn_bf, w_o_bf, wr_hi, wr_lo)


def _route_kernel(s_ref, rb_ref, w_ref):
    tr = s_ref.shape[1]
    neg = -jnp.inf
    sub = lax.broadcasted_iota(jnp.int32, (GROUP_SIZE, tr), 0)
    s_g = [s_ref[g * GROUP_SIZE:(g + 1) * GROUP_SIZE, :] for g in range(N_GROUPS)]
    sb_g = [s_g[g] + rb_ref[g * GROUP_SIZE:(g + 1) * GROUP_SIZE, :] for g in range(N_GROUPS)]

    gscore = []
    for g in range(N_GROUPS):
        xg = sb_g[g]
        top1 = jnp.max(xg, axis=0, keepdims=True)
        first = jnp.min(jnp.where(xg == top1, sub, GROUP_SIZE), axis=0, keepdims=True)
        top2 = jnp.max(jnp.where(sub == first, neg, xg), axis=0, keepdims=True)
        gscore.append(top1 + top2)

    masked = []
    for g in range(N_GROUPS):
        rank = jnp.zeros((1, tr), jnp.int32)
        for g2 in range(N_GROUPS):
            if g2 == g:
                continue
            ahead = (gscore[g2] >= gscore[g]) if g2 < g else (gscore[g2] > gscore[g])
            rank = rank + ahead.astype(jnp.int32)
        keep = jnp.broadcast_to(rank < TOPK_GROUPS, (GROUP_SIZE, tr))
        masked.append(jnp.where(keep, sb_g[g], neg))

    ranks = [jnp.zeros((GROUP_SIZE, tr), jnp.int32) for _ in range(N_GROUPS)]
    for g2 in range(N_GROUPS):
        for r2 in range(GROUP_SIZE):
            other = jnp.broadcast_to(masked[g2][r2:r2 + 1, :], (GROUP_SIZE, tr))
            for g in range(N_GROUPS):
                mine = masked[g]
                if g2 < g:
                    ahead = other >= mine
                elif g2 > g:
                    ahead = other > mine
                else:
                    ahead = (other > mine) | ((other == mine) & (sub > r2))
                ranks[g] = ranks[g] + ahead.astype(jnp.int32)

    picked = [jnp.where(ranks[g] < TOP_K, s_g[g], 0.0) for g in range(N_GROUPS)]
    total = picked[0]
    for g in range(1, N_GROUPS):
        total = total + picked[g]
    denom = jnp.sum(total, axis=0, keepdims=True)
    for g in range(N_GROUPS):
        w_ref[g * GROUP_SIZE:(g + 1) * GROUP_SIZE, :] = picked[g] / denom * ROUTED_SCALE
    w_ref[N_EXPERTS:N_EXPERTS + GROUP_SIZE, :] = jnp.where(sub == 0, 1.0, 0.0)
    w_ref[N_EXPERTS + GROUP_SIZE:, :] = jnp.zeros((WEIGHT_COLS - N_EXPERTS - GROUP_SIZE, tr), F32)


def _route(s_t, router_bias_col, tr):
    b, ne, l = s_t.shape
    return pl.pallas_call(
        _route_kernel,
        out_shape=jax.ShapeDtypeStruct((b, WEIGHT_COLS, l), F32),
        grid=(b, l // tr),
        in_specs=[pl.BlockSpec((None, ne, tr), lambda bi, i: (bi, 0, i)),
                  pl.BlockSpec((ne, 1), lambda bi, i: (0, 0))],
        out_specs=pl.BlockSpec((None, WEIGHT_COLS, tr), lambda bi, i: (bi, 0, i)),
        compiler_params=_cparams("parallel", "parallel"),
        name="route_topk",
    )(s_t, router_bias_col)


def _moe_kernel(h_ref, w_ref, x1_ref, mod_ref, g_ref, w1_ref, w3_ref, w2_ref, o_ref, acc_ref, *, ne):
    e = pl.program_id(2)

    @pl.when(e == 0)
    def _():
        acc_ref[...] = jnp.zeros_like(acc_ref)

    h = h_ref[...]
    a1 = _dot(h, w1_ref[...])
    a3 = _dot(h, w3_ref[...])
    w_hi, w_lo = _split_bf16(w_ref[...])
    pick = (lax.broadcasted_iota(jnp.int32, (2 * WEIGHT_COLS, D_EXPERT), 0) % WEIGHT_COLS == e)
    wcol = _dot(jnp.concatenate([w_hi, w_lo], axis=-1), pick.astype(BF16))
    hidden = (a1 * _sigmoid(a1)) * a3 * wcol
    acc_ref[...] += _dot(hidden.astype(BF16), w2_ref[...])

    @pl.when(e == ne - 1)
    def _():
        y = x1_ref[...] + mod_ref[5:6, :] * acc_ref[...]
        o_ref[...] = _rms(y, g_ref[...])


def _moe(h2, w_tok, x1, mod, final_g, w1_all, w3_all, w2_all, tm):
    b, l, d = x1.shape
    ne = w1_all.shape[0]
    row = pl.BlockSpec((None, tm, d), lambda bi, i, e: (bi, i, 0))
    return pl.pallas_call(
        functools.partial(_moe_kernel, ne=ne),
        out_shape=jax.ShapeDtypeStruct((b, l, d), F32),
        grid=(b, l // tm, ne),
        in_specs=[row,
                  pl.BlockSpec((None, tm, WEIGHT_COLS), lambda bi, i, e: (bi, i, 0)),
                  row,
                  pl.BlockSpec((None, 6, d), lambda bi, i, e: (bi, 0, 0)),
                  pl.BlockSpec((1, d), lambda bi, i, e: (0, 0)),
                  pl.BlockSpec((None, d, D_EXPERT), lambda bi, i, e: (e, 0, 0)),
                  pl.BlockSpec((None, d, D_EXPERT), lambda bi, i, e: (e, 0, 0)),
                  pl.BlockSpec((None, D_EXPERT, d), lambda bi, i, e: (e, 0, 0))],
        out_specs=row,
        scratch_shapes=[pltpu.VMEM((tm, d), F32)],
        compiler_params=_cparams("parallel", "parallel", "arbitrary"),
        name="moe_experts",
    )(h2, w_tok, x1, mod, final_g, w1_all, w3_all, w2_all)


def _t5_bucket_np(rel):
    half = N_BUCKETS // 2
    max_exact = half // 2
    ret = np.where(rel > 0, half, 0)
    n = np.abs(rel)
    nf = np.maximum(n, 1).astype(np.float32)
    large = max_exact + (np.log(nf / np.float32(max_exact)) / np.float32(math.log(MAX_DISTANCE / max_exact))
                         * np.float32(half - max_exact)).astype(np.int32)
    large = np.minimum(large, half - 1)
    return (ret + np.where(n < max_exact, n, large)).astype(np.int32)


def _bias_tables(rel_bias, t):
    assert t >= MAX_DISTANCE
    v = np.arange(2 * t)
    d = np.where(v < t, v, v - 2 * t)
    rel = np.stack([off - d for off in (-t, 0, t)])
    scaled = rel_bias.astype(F32) * LOG2E
    z = jnp.take(scaled, jnp.asarray(_t5_bucket_np(rel)), axis=0)
    z = jnp.transpose(z, (2, 0, 1))[:, :, None, :]
    far = jnp.stack([scaled[int(_t5_bucket_np(np.array(-MAX_DISTANCE)))],
                     scaled[int(_t5_bucket_np(np.array(MAX_DISTANCE)))]])
    return z, far


def _block_diag_gates(w_r, w_i):
    per = GATE_CHUNK // LRU_BLOCK
    eye = jnp.eye(per, dtype=F32)

    def bd(w):
        w = w.reshape(-1, per, LRU_BLOCK, LRU_BLOCK)
        return jnp.einsum('cjab,jk->cjakb', w, eye).reshape(-1, GATE_CHUNK, GATE_CHUNK)

    return jnp.concatenate([bd(w_r), bd(w_i)], axis=-1).astype(BF16)


def _tile(l, pref):
    return min(l, pref)


def _trunk(x, mod, p, tiles=None):
    b, l, d = x.shape
    tl = dict(inproj=512, scan=256, attn=1024, attn_rows=256, merge=512, route=512, moe=1024)
    if tiles:
        tl.update(tiles)
    tl = {k: _tile(l, v) for k, v in tl.items()}
    tl['attn_rows'] = min(tl['attn_rows'], tl['attn'])

    u_rnn, u_gate, q, k, vt, gm_a, gm_b = _inproj(x, mod, p['norm_mix_g'], p['w_main'], p['w_vt'],
                                                  tl['inproj'])

    h_fwd = _scan(u_rnn, p['conv_w'], p['conv_b'], p['wbd'][0], p['b_r'][0:1], p['b_i'][0:1],
                  p['lam'][0:1], tl['scan'])
    ga = _scan(u_rnn, p['conv_w'], p['conv_b'], p['wbd'][1], p['b_r'][1:2], p['b_i'][1:2],
               p['lam'][1:2], tl['scan'], h_fwd=h_fwd, u_gate=u_gate)

    bias_tiles, far = p['bias'](tl['attn'], tl['attn_rows'])
    o = _attention(q, k, vt, bias_tiles, far, p['lam_params'], p['subln_g'], tl['attn'], tl['attn_rows'])

    x1, h2, s_t = _merge(ga, o, gm_a, gm_b, x, mod, p['norm_ffn_g'], p['w_rnn_out'], p['w_attn_out'],
                         p['w_o'], p['wr_hi'], p['wr_lo'], tl['merge'])
    w_t = _route(s_t, p['router_bias'], tl['route'])
    w_tok = jnp.swapaxes(w_t, 1, 2)
    return _moe(h2, w_tok, x1, mod, p['final_norm_g'], p['w1'], p['w3'], p['w2'], tl['moe'])


def _prepare(norm_mix_g, norm_ffn_g, final_norm_g, w_in, conv_w, conv_b, w_rgate, b_rgate, w_igate,
             b_igate, lru_lambda, w_rnn_out, lambda_q1, lambda_k1, lambda_q2, lambda_k2, subln_g,
             rel_bias, w_attn_out, w_o, w_router, router_bias, w1_e, w3_e, w2_e, w1_s, w3_s, w2_s):
    wr_t = w_router[0].T.astype(F32)
    wr_hi = wr_t.astype(BF16)
    wr_lo = (wr_t - wr_hi.astype(F32)).astype(BF16)
    bias_cache = {}

    def bias(t, rc):
        if (t, rc) not in bias_cache:
            z, far = _bias_tables(rel_bias, t)
            bias_cache[(t, rc)] = (_bias_tiles(z, t, rc), far)
        return bias_cache[(t, rc)]

    w_in0 = w_in[0]
    v_lo = 2 * D_RNN + 2 * N_HEADS * 2 * HEAD_DIM
    v_hi = v_lo + N_HEADS * V_DIM
    return dict(
        norm_mix_g=norm_mix_g[0][None], norm_ffn_g=norm_ffn_g[0][None], final_norm_g=final_norm_g[None],
        w_main=jnp.concatenate([w_in0[:, :v_lo], w_in0[:, v_hi:]], axis=1).astype(BF16),
        w_vt=w_in0[:, v_lo:v_hi].T.astype(BF16),
        conv_w=conv_w[0], conv_b=conv_b[0][None],
        wbd=jnp.stack([_block_diag_gates(w_rgate[0, dr], w_igate[0, dr]) for dr in range(2)]),
        b_r=b_rgate[0], b_i=b_igate[0], lam=lru_lambda[0],
        w_rnn_out=w_rnn_out[0].astype(BF16), w_attn_out=w_attn_out[0].astype(BF16),
        w_o=w_o[0].astype(BF16),
        lam_params=jnp.stack([lambda_q1[0], lambda_k1[0], lambda_q2[0], lambda_k2[0]]).astype(F32),
        subln_g=subln_g[0][:, None].astype(F32), bias=bias,
        wr_hi=wr_hi, wr_lo=wr_lo, router_bias=router_bias[0][:, None].astype(F32),
        w1=jnp.concatenate([w1_e[0], w1_s[0][None]], axis=0).astype(BF16),
        w3=jnp.concatenate([w3_e[0], w3_s[0][None]], axis=0).astype(BF16),
        w2=jnp.concatenate([w2_e[0], w2_s[0][None]], axis=0).astype(BF16),
    )


def kernel(x_prompt, x_sample, c_prompt, c_sample, norm_mix_g, norm_ffn_g, final_norm_g, w_ada, b_ada, w_in, conv_w, conv_b, w_rgate, b_rgate, w_igate, b_igate, lru_lambda, w_rnn_out, lambda_q1, lambda_k1, lambda_q2, lambda_k2, subln_g, rel_bias, w_attn_out, w_o, w_router, router_bias, w1_e, w3_e, w2_e, w1_s, w3_s, w2_s):
    d = D_MODEL
    nb_p, nb_s = c_prompt.shape[0], c_sample.shape[0]
    pad = (-(nb_p + nb_s)) % SUBLANES
    c_all = jnp.concatenate([c_prompt, c_sample, jnp.zeros((pad, d), F32)], axis=0)
    mod = _ada(c_all, w_ada[0], b_ada[0][None]).reshape(-1, 6, d)
    p = _prepare(norm_mix_g, norm_ffn_g, final_norm_g, w_in, conv_w, conv_b, w_rgate, b_rgate,
                 w_igate, b_igate, lru_lambda, w_rnn_out, lambda_q1, lambda_k1, lambda_q2, lambda_k2,
                 subln_g, rel_bias, w_attn_out, w_o, w_router, router_bias, w1_e, w3_e, w2_e,
                 w1_s, w3_s, w2_s)
    y_prompt = _trunk(x_prompt, mod[:nb_p], p)
    y_sample = _trunk(x_sample, mod[nb_p:nb_p + nb_s], p)
    return (y_prompt, y_sample)
```

```python
import functools
import math

import numpy as np
import jax
import jax.numpy as jnp
from jax import lax
from jax.experimental import pallas as pl
from jax.experimental.pallas import tpu as pltpu

D_MODEL = 1024
D_RNN = 1024
LRU_BLOCK = 64
LRU_C = 8.0
N_HEADS = 8
HEAD_DIM = 64
V_DIM = 2 * HEAD_DIM
N_BUCKETS = 32
MAX_DISTANCE = 128
N_EXPERTS = 64
TOP_K = 8
N_GROUPS = 8
GROUP_SIZE = N_EXPERTS // N_GROUPS
TOPK_GROUPS = 4
D_EXPERT = 256
ROUTED_SCALE = 2.5
EPS = 1e-6
LAM_INIT = 0.8 - 0.6 * math.exp(-0.3 * 0)

VMEM_LIMIT_BYTES = 56 * 1024 * 1024
LANES = 128
SUBLANES = 8
GATE_CHUNK = 256
KEY_BLOCK = 256
PIPE_SKEW = 2
S_BUFFERS = 4
V_ROWS = V_DIM + 16
LOG2E = math.log2(math.e)
WEIGHT_COLS = 128

F32 = jnp.float32
BF16 = jnp.bfloat16


def _cparams(*sem):
    return pltpu.CompilerParams(dimension_semantics=sem, vmem_limit_bytes=VMEM_LIMIT_BYTES)


def _dot(a, b):
    return jnp.dot(a, b, preferred_element_type=F32)


def _dot_nt(a, b):
    return lax.dot_general(a, b, (((1,), (1,)), ((), ())), preferred_element_type=F32)


def _split_bf16(x):
    hi = x.astype(BF16)
    lo = (x - hi.astype(F32)).astype(BF16)
    return hi, lo


def _sigmoid(x):
    return 1.0 / (1.0 + jnp.exp(-x))


def _rms(x, g):
    return x * lax.rsqrt(jnp.mean(x * x, axis=-1, keepdims=True) + EPS) * g


def _ada_kernel(c_ref, w_ref, b_ref, o_ref):
    c = c_ref[...]
    sc = c * _sigmoid(c)
    c_hi, c_lo = _split_bf16(sc)
    w_hi, w_lo = _split_bf16(w_ref[...])
    o_ref[...] = _dot(c_hi, w_hi) + _dot(c_lo, w_hi) + _dot(c_hi, w_lo) + b_ref[...]


def _ada(c_all, w_ada, b_ada, tn=1536):
    rows, d = c_all.shape
    n = w_ada.shape[1]
    return pl.pallas_call(
        _ada_kernel,
        out_shape=jax.ShapeDtypeStruct((rows, n), F32),
        grid=(n // tn,),
        in_specs=[pl.BlockSpec((rows, d), lambda j: (0, 0)),
                  pl.BlockSpec((d, tn), lambda j: (0, j)),
                  pl.BlockSpec((1, tn), lambda j: (0, j))],
        out_specs=pl.BlockSpec((rows, tn), lambda j: (0, j)),
        compiler_params=_cparams("parallel"),
        name="ada_mod",
    )(c_all, w_ada, b_ada)


def _inproj_kernel(x_ref, mod_ref, g_ref, w_ref, wvt_ref, urnn_ref, ugate_ref, q_ref, k_ref, vt_ref,
                   ga_ref, gb_ref):
    d = D_MODEL
    x = x_ref[...]
    h = _rms(x, g_ref[...]) * (1.0 + mod_ref[1:2, :]) + mod_ref[0:1, :]
    hb = h.astype(BF16)

    def col(c):
        return _dot(hb, w_ref[:, c * d:(c + 1) * d])

    urnn_ref[...] = col(0)
    ugate_ref[...] = col(1)
    q = (col(2) * (HEAD_DIM ** -0.5 * LOG2E)).astype(BF16)
    k = col(3).astype(BF16)
    vt = _dot_nt(wvt_ref[...], hb).astype(BF16)
    for hd in range(N_HEADS):
        sl = slice(hd * V_DIM, (hd + 1) * V_DIM)
        q_ref[hd] = q[:, sl]
        k_ref[hd] = k[:, sl]
        vt_ref[hd] = vt[sl, :]
    ga_ref[...] = _sigmoid(col(4))
    gb_ref[...] = _sigmoid(col(5))


def _inproj(x, mod, g, w_main_bf, w_vt_bf, tm):
    b, l, d = x.shape
    ncol = w_main_bf.shape[1]
    row = pl.BlockSpec((None, tm, d), lambda bi, i: (bi, i, 0))
    head = pl.BlockSpec((None, N_HEADS, tm, V_DIM), lambda bi, i: (bi, 0, i, 0))
    head_t = pl.BlockSpec((None, N_HEADS, V_DIM, tm), lambda bi, i: (bi, 0, 0, i))
    f32_out = jax.ShapeDtypeStruct((b, l, d), F32)
    head_out = jax.ShapeDtypeStruct((b, N_HEADS, l, V_DIM), BF16)
    head_t_out = jax.ShapeDtypeStruct((b, N_HEADS, V_DIM, l), BF16)
    return pl.pallas_call(
        _inproj_kernel,
        out_shape=(f32_out, f32_out, head_out, head_out, head_t_out, f32_out, f32_out),
        grid=(b, l // tm),
        in_specs=[row,
                  pl.BlockSpec((None, 6, d), lambda bi, i: (bi, 0, 0)),
                  pl.BlockSpec((1, d), lambda bi, i: (0, 0)),
                  pl.BlockSpec((d, ncol), lambda bi, i: (0, 0), pipeline_mode=pl.Buffered(1)),
                  pl.BlockSpec((d, d), lambda bi, i: (0, 0), pipeline_mode=pl.Buffered(1))],
        out_specs=(row, row, head, head, head_t, row, row),
        compiler_params=_cparams("parallel", "parallel"),
        name="in_proj",
    )(x, mod, g, w_main_bf, w_vt_bf)


def _gelu_tanh(x):
    return x * (0.5 * (1.0 + jnp.tanh(math.sqrt(2.0 / math.pi) * (x + 0.044715 * (x * x * x)))))


def _scan_kernel(*refs, tl, nt, reverse):
    if reverse:
        (u_ref, prev_ref, next_ref, cw_ref, cb_ref, wbd_ref, br_ref, bi_ref, lam_ref,
         hf_ref, ug_ref, o_ref, carry_ref, h_ref) = refs
    else:
        (u_ref, prev_ref, next_ref, cw_ref, cb_ref, wbd_ref, br_ref, bi_ref, lam_ref,
         o_ref, carry_ref, h_ref) = refs
    c = D_RNN
    step = pl.program_id(1)
    t = (nt - 1 - step) if reverse else step

    @pl.when(step == 0)
    def _():
        carry_ref[...] = jnp.zeros_like(carry_ref)

    u = u_ref[...]
    row = lax.broadcasted_iota(jnp.int32, (tl, c), 0)
    halo_lo = jnp.where(t > 0, prev_ref[...], 0.0)
    halo_hi = jnp.where(t < nt - 1, next_ref[...], 0.0)
    xc = cb_ref[...] + cw_ref[2:3, :] * u
    for k in (2, 1):
        lo_fill = jnp.concatenate([pltpu.roll(halo_lo, k, 0)] + [halo_lo] * (tl // SUBLANES - 1), axis=0)
        shifted = jnp.where(row < k, lo_fill, pltpu.roll(u, k, 0))
        xc = xc + cw_ref[2 - k:3 - k, :] * shifted
    hi_fill = jnp.concatenate([halo_hi] * (tl // SUBLANES - 1) + [pltpu.roll(halo_hi, SUBLANES - 1, 0)],
                              axis=0)
    shifted = jnp.where(row >= tl - 1, hi_fill, pltpu.roll(u, tl - 1, 0))
    xc = xc + cw_ref[3:4, :] * shifted

    xcb = xc.astype(BF16)
    r_parts, i_parts = [], []
    for ch in range(c // GATE_CHUNK):
        z = _dot(xcb[:, ch * GATE_CHUNK:(ch + 1) * GATE_CHUNK], wbd_ref[ch])
        r_parts.append(z[:, :GATE_CHUNK])
        i_parts.append(z[:, GATE_CHUNK:])
    r = _sigmoid(jnp.concatenate(r_parts, axis=-1) + br_ref[...])
    ig = _sigmoid(jnp.concatenate(i_parts, axis=-1) + bi_ref[...])
    nl = -lam_ref[...]
    softplus = jnp.maximum(nl, 0.0) + jnp.log1p(jnp.exp(-jnp.abs(nl)))
    log_a = (-LRU_C) * r * softplus
    a = jnp.exp(log_a)
    one_m_a2 = -jnp.tanh(log_a) * (1.0 + a * a)
    root = jnp.where(one_m_a2 > 0.0, one_m_a2 * lax.rsqrt(one_m_a2), 0.0)
    bb = root * (ig * xc)

    ngroups = tl // SUBLANES
    a = a.reshape(ngroups, SUBLANES, c)
    bb = bb.reshape(ngroups, SUBLANES, c)
    pos = lax.broadcasted_iota(jnp.int32, (ngroups, SUBLANES, c), 1)
    s = 1
    while s < SUBLANES:
        if reverse:
            valid = pos < SUBLANES - s
            shift = SUBLANES - s
        else:
            valid = pos >= s
            shift = s
        a_n = jnp.where(valid, pltpu.roll(a, shift, 1), 1.0)
        b_n = jnp.where(valid, pltpu.roll(bb, shift, 1), 0.0)
        bb = a * b_n + bb
        a = a * a_n
        s *= 2
    carry = carry_ref[...]
    for step_g in range(ngroups):
        g = (ngroups - 1 - step_g) if reverse else step_g
        h_g = a[g] * carry + bb[g]
        h_ref[g * SUBLANES:(g + 1) * SUBLANES, :] = h_g
        carry = h_g[0:1, :] if reverse else h_g[SUBLANES - 1:SUBLANES, :]
    carry_ref[...] = carry

    if reverse:
        o_ref[...] = (_gelu_tanh(ug_ref[...]) * (hf_ref[...] + h_ref[...])).astype(o_ref.dtype)
    else:
        o_ref[...] = h_ref[...]


def _scan(u, conv_w, conv_b, wbd, b_r, b_i, lam, tl, h_fwd=None, u_gate=None):
    b, l, c = u.shape
    nt = l // tl
    reverse = h_fwd is not None
    tpb = tl // SUBLANES
    nblk = l // SUBLANES

    def tix(s):
        return (nt - 1 - s) if reverse else s

    row = pl.BlockSpec((None, tl, c), lambda bi, s: (bi, tix(s), 0))
    prev = pl.BlockSpec((None, SUBLANES, c), lambda bi, s: (bi, jnp.maximum(tix(s) * tpb - 1, 0), 0))
    nxt = pl.BlockSpec((None, SUBLANES, c),
                       lambda bi, s: (bi, jnp.minimum((tix(s) + 1) * tpb, nblk - 1), 0))
    vec = pl.BlockSpec((1, c), lambda bi, s: (0, 0))
    in_specs = [row, prev, nxt,
                pl.BlockSpec((4, c), lambda bi, s: (0, 0)), vec,
                pl.BlockSpec(wbd.shape, lambda bi, s: (0, 0, 0)), vec, vec, vec]
    args = [u, u, u, conv_w, conv_b, wbd, b_r, b_i, lam]
    if reverse:
        in_specs += [row, row]
        args += [h_fwd, u_gate]
        out_dtype = BF16
    else:
        out_dtype = F32
    return pl.pallas_call(
        functools.partial(_scan_kernel, tl=tl, nt=nt, reverse=reverse),
        out_shape=jax.ShapeDtypeStruct((b, l, c), out_dtype),
        grid=(b, nt),
        in_specs=in_specs,
        out_specs=row,
        scratch_shapes=[pltpu.VMEM((1, c), F32), pltpu.VMEM((tl, c), F32)],
        compiler_params=_cparams("parallel", "arbitrary"),
        name="rglru_bwd" if reverse else "rglru_fwd",
    )(*args)


def _bias_kernel(z_ref, o_ref, *, t, rc):
    y = jnp.broadcast_to(z_ref[...], (LANES, 2 * t))
    row = lax.broadcasted_iota(jnp.int32, (LANES, 2 * t), 0)
    bit = 0
    while (1 << bit) < LANES:
        y = jnp.where(((row >> bit) & 1) == 1, pltpu.roll(y, 1 << bit, 1), y)
        bit += 1
    for rh in range(t // LANES):
        blk = pltpu.roll(y, LANES * rh, 1) if rh else y
        for c in range(t // rc):
            o_ref[c, rh * LANES:(rh + 1) * LANES, :] = blk[:, c * rc:(c + 1) * rc]


def _bias_tiles(z, t, rc):
    nh = z.shape[0]
    return pl.pallas_call(
        functools.partial(_bias_kernel, t=t, rc=rc),
        out_shape=jax.ShapeDtypeStruct((nh, 3, t // rc, t, rc), F32),
        grid=(nh, 3),
        in_specs=[pl.BlockSpec((None, None, 1, 2 * t), lambda h, o: (h, o, 0, 0))],
        out_specs=pl.BlockSpec((None, None, t // rc, t, rc), lambda h, o: (h, o, 0, 0, 0)),
        compiler_params=_cparams("parallel", "parallel"),
        name="bias_tiles",
    )(z)


def _reduce_row_groups(s, op):
    parts = [s[r:r + SUBLANES, :] for r in range(0, s.shape[0], SUBLANES)]
    while len(parts) > 1:
        parts = [op(a, b) for a, b in zip(parts[0::2], parts[1::2])] + \
                ([parts[-1]] if len(parts) % 2 else [])
    return parts[0]


def _attn_kernel(far_ref, q_ref, k_ref, vt_ref, bias_ref, lamp_ref, g_ref, o_ref,
                 qz_ref, vta_ref, m_ref, alpha_ref, acc_ref, s_ref, *, nk, rc):
    hd = pl.program_id(1)
    i = pl.program_id(2)
    j = pl.program_id(3)
    t = q_ref.shape[0]
    nc = t // rc

    nunits = 2 * nc
    nkb = t // KEY_BLOCK
    assert nunits % S_BUFFERS == 0 and PIPE_SKEW < S_BUFFERS
    cur = j % 2
    prev = 1 - cur

    @pl.when(j == 0)
    def _():
        q = q_ref[...]
        lane = lax.broadcasted_iota(jnp.int32, q.shape, 1)
        qz_ref[0] = jnp.where(lane < HEAD_DIM, q, jnp.zeros_like(q))
        qz_ref[1] = jnp.where(lane >= HEAD_DIM, q, jnp.zeros_like(q))
        m_ref[...] = jnp.full_like(m_ref, -jnp.inf)
        alpha_ref[...] = jnp.ones_like(alpha_ref)
        acc_ref[...] = jnp.zeros_like(acc_ref)
        s_ref[...] = jnp.zeros_like(s_ref)
        ones_row = lax.broadcasted_iota(jnp.int32, (V_ROWS - V_DIM, t), 0) == 0
        for side in range(2):
            vta_ref[side, 0:V_DIM, :] = jnp.zeros((V_DIM, t), BF16)
            vta_ref[side, V_DIM:, :] = jnp.where(ones_row, 1.0, 0.0).astype(BF16)

    vta_ref[cur, 0:V_DIM, :] = vt_ref[...]

    shift = (jnp.where(j == jnp.maximum(i - 1, 0), far_ref[0, hd], 0.0)
             - jnp.where(j == jnp.minimum(i + 1, nk - 1) + 1, far_ref[1, hd], 0.0))

    def value_stage_begin(u, carried):
        c, br = divmod(u, 2)
        m_u = m_ref[br, c]
        if carried:
            m_u = jnp.where(j > 0, m_u, jnp.inf)
        return m_u

    def value_stage_block(u, m_u, side, ks, accp):
        p = jnp.exp2(s_ref[u % S_BUFFERS, ks, :] - m_u)
        pv = _dot(vta_ref[side, :, ks], p.astype(BF16))
        return pv if accp is None else accp + pv

    def value_stage_end(u, accp):
        c, br = divmod(u, 2)
        acc_ref[br, c] = alpha_ref[br, c] * acc_ref[br, c] + accp

    def step(with_bias):
        for slot in range(nunits):
            ca, bra = divmod(slot, 2)
            qa = qz_ref[bra, ca * rc:(ca + 1) * rc, :]
            colmax = None
            carried = slot < PIPE_SKEW
            ub = slot - PIPE_SKEW + (nunits if carried else 0)
            side = prev if carried else cur
            m_b = value_stage_begin(ub, carried)
            accp = None
            for kb in range(nkb):
                ks = slice(kb * KEY_BLOCK, (kb + 1) * KEY_BLOCK)
                s = _dot_nt(k_ref[ks, :], qa)
                if with_bias:
                    s = s + bias_ref[ca, ks, :]
                s_ref[slot % S_BUFFERS, ks, :] = s
                bmax = _reduce_row_groups(s, jnp.maximum)
                colmax = bmax if colmax is None else jnp.maximum(colmax, bmax)
                accp = value_stage_block(ub, m_b, side, ks, accp)
            m_old = m_ref[bra, ca] + shift
            m_new = jnp.maximum(m_old, jnp.max(colmax, axis=0, keepdims=True))
            alpha_ref[bra, ca] = jnp.exp2(m_old - m_new)
            m_ref[bra, ca] = m_new
            value_stage_end(ub, accp)

    is_band = jnp.abs(j - i) <= 1

    @pl.when(is_band)
    def _():
        step(True)

    @pl.when(jnp.logical_not(is_band))
    def _():
        step(False)

    @pl.when(j == nk - 1)
    def _():
        for u in range(nunits - PIPE_SKEW, nunits):
            m_u = value_stage_begin(u, False)
            accp = None
            for kb in range(nkb):
                accp = value_stage_block(u, m_u, cur, slice(kb * KEY_BLOCK, (kb + 1) * KEY_BLOCK), accp)
            value_stage_end(u, accp)
        lp = lamp_ref[...]
        lam = (jnp.exp(jnp.sum(lp[0:1, :] * lp[1:2, :], axis=-1, keepdims=True))
               - jnp.exp(jnp.sum(lp[2:3, :] * lp[3:4, :], axis=-1, keepdims=True)) + LAM_INIT)
        for c in range(nc):
            o = (acc_ref[0, c, 0:V_DIM, :] / acc_ref[0, c, V_DIM:V_DIM + 1, :]
                 - lam * (acc_ref[1, c, 0:V_DIM, :] / acc_ref[1, c, V_DIM:V_DIM + 1, :]))
            o = o * lax.rsqrt(jnp.mean(o * o, axis=0, keepdims=True) + EPS) * g_ref[...]
            o = o * (1.0 - LAM_INIT)
            o_ref[c * rc:(c + 1) * rc, :] = o.T.astype(o_ref.dtype)


def _attention(q, k, vt, bias_tiles, far_bias, lam_params, subln_g_col, t, rc):
    b, nh, l, dv = q.shape
    nq = l // t
    nc = t // rc
    qspec = pl.BlockSpec((None, None, t, dv), lambda bi, h, i, j: (bi, h, i, 0))
    kspec = pl.BlockSpec((None, None, t, dv), lambda bi, h, i, j: (bi, h, j, 0))
    vspec = pl.BlockSpec((None, None, dv, t), lambda bi, h, i, j: (bi, h, 0, j))
    return pl.pallas_call(
        functools.partial(_attn_kernel, nk=nq, rc=rc),
        out_shape=jax.ShapeDtypeStruct((b, nh, l, dv), BF16),
        grid=(b, nh, nq, nq),
        in_specs=[pl.BlockSpec(memory_space=pltpu.SMEM),
                  qspec, kspec, vspec,
                  pl.BlockSpec((None, None, nc, t, rc),
                               lambda bi, h, i, j: (h, jnp.clip(j - i, -1, 1) + 1, 0, 0, 0)),
                  pl.BlockSpec((4, HEAD_DIM), lambda bi, h, i, j: (0, 0)),
                  pl.BlockSpec((dv, 1), lambda bi, h, i, j: (0, 0))],
        out_specs=qspec,
        scratch_shapes=[pltpu.VMEM((2, t, dv), BF16), pltpu.VMEM((2, V_ROWS, t), BF16),
                        pltpu.VMEM((2, nc, 1, rc), F32), pltpu.VMEM((2, nc, 1, rc), F32),
                        pltpu.VMEM((2, nc, V_ROWS, rc), F32),
                        pltpu.VMEM((S_BUFFERS, t, rc), F32)],
        compiler_params=_cparams("parallel", "parallel", "parallel", "arbitrary"),
        name="diff_attn",
    )(far_bias, q, k, vt, bias_tiles, lam_params, subln_g_col)


def _merge_kernel(ga_ref, o_ref, gma_ref, gmb_ref, x_ref, mod_ref, g_ref, wrnn_ref, wattn_ref,
                  wo_ref, wrh_ref, wrl_ref, x1_ref, h2_ref, s_ref):
    ya = _dot(ga_ref[...], wrnn_ref[...])
    ob = jnp.concatenate([o_ref[hd] for hd in range(N_HEADS)], axis=-1)
    yb = _dot(ob, wattn_ref[...])
    mixed = (gma_ref[...] * ya + gmb_ref[...] * yb).astype(BF16)
    x1 = x_ref[...] + mod_ref[2:3, :] * _dot(mixed, wo_ref[...])
    x1_ref[...] = x1
    h2 = _rms(x1, g_ref[...]) * (1.0 + mod_ref[4:5, :]) + mod_ref[3:4, :]
    h2_hi, h2_lo = _split_bf16(h2)
    h2_ref[...] = h2_hi
    logits = (_dot_nt(wrh_ref[...], h2_hi) + _dot_nt(wrh_ref[...], h2_lo)
              + _dot_nt(wrl_ref[...], h2_hi))
    s_ref[...] = _sigmoid(logits)


def _merge(ga, o, gm_a, gm_b, x, mod, g, w_rnn_bf, w_attn_bf, w_o_bf, wr_hi, wr_lo, tm):
    b, l, d = x.shape
    row = pl.BlockSpec((None, tm, d), lambda bi, i: (bi, i, 0))
    wspec = pl.BlockSpec((d, d), lambda bi, i: (0, 0), pipeline_mode=pl.Buffered(1))
    rspec = pl.BlockSpec((N_EXPERTS, d), lambda bi, i: (0, 0))
    return pl.pallas_call(
        _merge_kernel,
        out_shape=(jax.ShapeDtypeStruct((b, l, d), F32),
                   jax.ShapeDtypeStruct((b, l, d), BF16),
                   jax.ShapeDtypeStruct((b, N_EXPERTS, l), F32)),
        grid=(b, l // tm),
        in_specs=[row,
                  pl.BlockSpec((None, N_HEADS, tm, V_DIM), lambda bi, i: (bi, 0, i, 0)),
                  row, row, row,
                  pl.BlockSpec((None, 6, d), lambda bi, i: (bi, 0, 0)),
                  pl.BlockSpec((1, d), lambda bi, i: (0, 0)),
                  wspec, wspec, wspec, rspec, rspec],
        out_specs=(row, row, pl.BlockSpec((None, N_EXPERTS, tm), lambda bi, i: (bi, 0, i))),
        compiler_params=_cparams("parallel", "parallel"),
        name="merge_router",
    )(ga, o, gm_a, gm_b, x, mod, g, w_rnn_bf, w_attn_bf, w_o_bf, wr_hi, wr_lo)


def _route_kernel(s_ref, rb_ref, w_ref):
    tr = s_ref.shape[1]
    neg = -jnp.inf
    sub = lax.broadcasted_iota(jnp.int32, (GROUP_SIZE, tr), 0)
    s_g = [s_ref[g * GROUP_SIZE:(g + 1) * GROUP_SIZE, :] for g in range(N_GROUPS)]
    sb_g = [s_g[g] + rb_ref[g * GROUP_SIZE:(g + 1) * GROUP_SIZE, :] for g in range(N_GROUPS)]

    gscore = []
    for g in range(N_GROUPS):
        xg = sb_g[g]
        top1 = jnp.max(xg, axis=0, keepdims=True)
        first = jnp.min(jnp.where(xg == top1, sub, GROUP_SIZE), axis=0, keepdims=True)
        top2 = jnp.max(jnp.where(sub == first, neg, xg), axis=0, keepdims=True)
        gscore.append(top1 + top2)

    masked = []
    for g in range(N_GROUPS):
        rank = jnp.zeros((1, tr), jnp.int32)
        for g2 in range(N_GROUPS):
            if g2 == g:
                continue
            ahead = (gscore[g2] >= gscore[g]) if g2 < g else (gscore[g2] > gscore[g])
            rank = rank + ahead.astype(jnp.int32)
        keep = jnp.broadcast_to(rank < TOPK_GROUPS, (GROUP_SIZE, tr))
        masked.append(jnp.where(keep, sb_g[g], neg))

    ranks = [jnp.zeros((GROUP_SIZE, tr), jnp.int32) for _ in range(N_GROUPS)]
    for g2 in range(N_GROUPS):
        for r2 in range(GROUP_SIZE):
            other = jnp.broadcast_to(masked[g2][r2:r2 + 1, :], (GROUP_SIZE, tr))
            for g in range(N_GROUPS):
                mine = masked[g]
                if g2 < g:
                    ahead = other >= mine
                elif g2 > g:
                    ahead = other > mine
                else:
                    ahead = (other > mine) | ((other == mine) & (sub > r2))
                ranks[g] = ranks[g] + ahead.astype(jnp.int32)

    picked = [jnp.where(ranks[g] < TOP_K, s_g[g], 0.0) for g in range(N_GROUPS)]
    total = picked[0]
    for g in range(1, N_GROUPS):
        total = total + picked[g]
    denom = jnp.sum(total, axis=0, keepdims=True)
    for g in range(N_GROUPS):
        w_ref[g * GROUP_SIZE:(g + 1) * GROUP_SIZE, :] = picked[g] / denom * ROUTED_SCALE
    w_ref[N_EXPERTS:N_EXPERTS + GROUP_SIZE, :] = jnp.where(sub == 0, 1.0, 0.0)
    w_ref[N_EXPERTS + GROUP_SIZE:, :] = jnp.zeros((WEIGHT_COLS - N_EXPERTS - GROUP_SIZE, tr), F32)


def _route(s_t, router_bias_col, tr):
    b, ne, l = s_t.shape
    return pl.pallas_call(
        _route_kernel,
        out_shape=jax.ShapeDtypeStruct((b, WEIGHT_COLS, l), F32),
        grid=(b, l // tr),
        in_specs=[pl.BlockSpec((None, ne, tr), lambda bi, i: (bi, 0, i)),
                  pl.BlockSpec((ne, 1), lambda bi, i: (0, 0))],
        out_specs=pl.BlockSpec((None, WEIGHT_COLS, tr), lambda bi, i: (bi, 0, i)),
        compiler_params=_cparams("parallel", "parallel"),
        name="route_topk",
    )(s_t, router_bias_col)


def _moe_kernel(h_ref, w_ref, x1_ref, mod_ref, g_ref, w1_ref, w3_ref, w2_ref, o_ref, acc_ref, *, ne):
    e = pl.program_id(2)

    @pl.when(e == 0)
    def _():
        acc_ref[...] = jnp.zeros_like(acc_ref)

    h = h_ref[...]
    a1 = _dot(h, w1_ref[...])
    a3 = _dot(h, w3_ref[...])
    w = w_ref[...]
    lane = lax.broadcasted_iota(jnp.int32, w.shape, 1)
    wcol = jnp.sum(jnp.where(lane == e, w, 0.0), axis=-1, keepdims=True)
    hidden = (a1 * _sigmoid(a1)) * a3 * wcol
    acc_ref[...] += _dot(hidden.astype(BF16), w2_ref[...])

    @pl.when(e == ne - 1)
    def _():
        y = x1_ref[...] + mod_ref[5:6, :] * acc_ref[...]
        o_ref[...] = _rms(y, g_ref[...])


def _moe(h2, w_tok, x1, mod, final_g, w1_all, w3_all, w2_all, tm):
    b, l, d = x1.shape
    ne = w1_all.shape[0]
    row = pl.BlockSpec((None, tm, d), lambda bi, i, e: (bi, i, 0))
    return pl.pallas_call(
        functools.partial(_moe_kernel, ne=ne),
        out_shape=jax.ShapeDtypeStruct((b, l, d), F32),
        grid=(b, l // tm, ne),
        in_specs=[row,
                  pl.BlockSpec((None, tm, WEIGHT_COLS), lambda bi, i, e: (bi, i, 0)),
                  row,
                  pl.BlockSpec((None, 6, d), lambda bi, i, e: (bi, 0, 0)),
                  pl.BlockSpec((1, d), lambda bi, i, e: (0, 0)),
                  pl.BlockSpec((None, d, D_EXPERT), lambda bi, i, e: (e, 0, 0)),
                  pl.BlockSpec((None, d, D_EXPERT), lambda bi, i, e: (e, 0, 0)),
                  pl.BlockSpec((None, D_EXPERT, d), lambda bi, i, e: (e, 0, 0))],
        out_specs=row,
        scratch_shapes=[pltpu.VMEM((tm, d), F32)],
        compiler_params=_cparams("parallel", "parallel", "arbitrary"),
        name="moe_experts",
    )(h2, w_tok, x1, mod, final_g, w1_all, w3_all, w2_all)


def _t5_bucket_np(rel):
    half = N_BUCKETS // 2
    max_exact = half // 2
    ret = np.where(rel > 0, half, 0)
    n = np.abs(rel)
    nf = np.maximum(n, 1).astype(np.float32)
    large = max_exact + (np.log(nf / np.float32(max_exact)) / np.float32(math.log(MAX_DISTANCE / max_exact))
                         * np.float32(half - max_exact)).astype(np.int32)
    large = np.minimum(large, half - 1)
    return (ret + np.where(n < max_exact, n, large)).astype(np.int32)


def _bias_tables(rel_bias, t):
    assert t >= MAX_DISTANCE
    v = np.arange(2 * t)
    d = np.where(v < t, v, v - 2 * t)
    rel = np.stack([off - d for off in (-t, 0, t)])
    scaled = rel_bias.astype(F32) * LOG2E
    z = jnp.take(scaled, jnp.asarray(_t5_bucket_np(rel)), axis=0)
    z = jnp.transpose(z, (2, 0, 1))[:, :, None, :]
    far = jnp.stack([scaled[int(_t5_bucket_np(np.array(-MAX_DISTANCE)))],
                     scaled[int(_t5_bucket_np(np.array(MAX_DISTANCE)))]])
    return z, far


def _block_diag_gates(w_r, w_i):
    per = GATE_CHUNK // LRU_BLOCK
    eye = jnp.eye(per, dtype=F32)

    def bd(w):
        w = w.reshape(-1, per, LRU_BLOCK, LRU_BLOCK)
        return jnp.einsum('cjab,jk->cjakb', w, eye).reshape(-1, GATE_CHUNK, GATE_CHUNK)

    return jnp.concatenate([bd(w_r), bd(w_i)], axis=-1).astype(BF16)


def _tile(l, pref):
    return min(l, pref)


def _trunk(x, mod, p, tiles=None):
    b, l, d = x.shape
    tl = dict(inproj=512, scan=256, attn=1024, attn_rows=256, merge=512, route=512, moe=1024)
    if tiles:
        tl.update(tiles)
    tl = {k: _tile(l, v) for k, v in tl.items()}
    tl['attn_rows'] = min(tl['attn_rows'], tl['attn'])

    u_rnn, u_gate, q, k, vt, gm_a, gm_b = _inproj(x, mod, p['norm_mix_g'], p['w_main'], p['w_vt'],
                                                  tl['inproj'])

    h_fwd = _scan(u_rnn, p['conv_w'], p['conv_b'], p['wbd'][0], p['b_r'][0:1], p['b_i'][0:1],
                  p['lam'][0:1], tl['scan'])
    ga = _scan(u_rnn, p['conv_w'], p['conv_b'], p['wbd'][1], p['b_r'][1:2], p['b_i'][1:2],
               p['lam'][1:2], tl['scan'], h_fwd=h_fwd, u_gate=u_gate)

    bias_tiles, far = p['bias'](tl['attn'], tl['attn_rows'])
    o = _attention(q, k, vt, bias_tiles, far, p['lam_params'], p['subln_g'], tl['attn'], tl['attn_rows'])

    x1, h2, s_t = _merge(ga, o, gm_a, gm_b, x, mod, p['norm_ffn_g'], p['w_rnn_out'], p['w_attn_out'],
                         p['w_o'], p['wr_hi'], p['wr_lo'], tl['merge'])
    w_t = _route(s_t, p['router_bias'], tl['route'])
    w_tok = jnp.swapaxes(w_t, 1, 2)
    return _moe(h2, w_tok, x1, mod, p['final_norm_g'], p['w1'], p['w3'], p['w2'], tl['moe'])


def _prepare(norm_mix_g, norm_ffn_g, final_norm_g, w_in, conv_w, conv_b, w_rgate, b_rgate, w_igate,
             b_igate, lru_lambda, w_rnn_out, lambda_q1, lambda_k1, lambda_q2, lambda_k2, subln_g,
             rel_bias, w_attn_out, w_o, w_router, router_bias, w1_e, w3_e, w2_e, w1_s, w3_s, w2_s):
    wr_t = w_router[0].T.astype(F32)
    wr_hi = wr_t.astype(BF16)
    wr_lo = (wr_t - wr_hi.astype(F32)).astype(BF16)
    bias_cache = {}

    def bias(t, rc):
        if (t, rc) not in bias_cache:
            z, far = _bias_tables(rel_bias, t)
            bias_cache[(t, rc)] = (_bias_tiles(z, t, rc), far)
        return bias_cache[(t, rc)]

    w_in0 = w_in[0]
    v_lo = 2 * D_RNN + 2 * N_HEADS * 2 * HEAD_DIM
    v_hi = v_lo + N_HEADS * V_DIM
    return dict(
        norm_mix_g=norm_mix_g[0][None], norm_ffn_g=norm_ffn_g[0][None], final_norm_g=final_norm_g[None],
        w_main=jnp.concatenate([w_in0[:, :v_lo], w_in0[:, v_hi:]], axis=1).astype(BF16),
        w_vt=w_in0[:, v_lo:v_hi].T.astype(BF16),
        conv_w=conv_w[0], conv_b=conv_b[0][None],
        wbd=jnp.stack([_block_diag_gates(w_rgate[0, dr], w_igate[0, dr]) for dr in range(2)]),
        b_r=b_rgate[0], b_i=b_igate[0], lam=lru_lambda[0],
        w_rnn_out=w_rnn_out[0].astype(BF16), w_attn_out=w_attn_out[0].astype(BF16),
        w_o=w_o[0].astype(BF16),
        lam_params=jnp.stack([lambda_q1[0], lambda_k1[0], lambda_q2[0], lambda_k2[0]]).astype(F32),
        subln_g=subln_g[0][:, None].astype(F32), bias=bias,
        wr_hi=wr_hi, wr_lo=wr_lo, router_bias=router_bias[0][:, None].astype(F32),
        w1=jnp.concatenate([w1_e[0], w1_s[0][None]], axis=0).astype(BF16),
        w3=jnp.concatenate([w3_e[0], w3_s[0][None]], axis=0).astype(BF16),
        w2=jnp.concatenate([w2_e[0], w2_s[0][None]], axis=0).astype(BF16),
    )


def kernel(x_prompt, x_sample, c_prompt, c_sample, norm_mix_g, norm_ffn_g, final_norm_g, w_ada, b_ada, w_in, conv_w, conv_b, w_rgate, b_rgate, w_igate, b_igate, lru_lambda, w_rnn_out, lambda_q1, lambda_k1, lambda_q2, lambda_k2, subln_g, rel_bias, w_attn_out, w_o, w_router, router_bias, w1_e, w3_e, w2_e, w1_s, w3_s, w2_s):
    d = D_MODEL
    nb_p, nb_s = c_prompt.shape[0], c_sample.shape[0]
    pad = (-(nb_p + nb_s)) % SUBLANES
    c_all = jnp.concatenate([c_prompt, c_sample, jnp.zeros((pad, d), F32)], axis=0)
    mod = _ada(c_all, w_ada[0], b_ada[0][None]).reshape(-1, 6, d)
    p = _prepare(norm_mix_g, norm_ffn_g, final_norm_g, w_in, conv_w, conv_b, w_rgate, b_rgate,
                 w_igate, b_igate, lru_lambda, w_rnn_out, lambda_q1, lambda_k1, lambda_q2, lambda_k2,
                 subln_g, rel_bias, w_attn_out, w_o, w_router, router_bias, w1_e, w3_e, w2_e,
                 w1_s, w3_s, w2_s)
    y_prompt = _trunk(x_prompt, mod[:nb_p], p)
    y_sample = _trunk(x_sample, mod[nb_p:nb_p + nb_s], p)
    return (y_prompt, y_sample)
```

```python
import functools
import math

import numpy as np
import jax
import jax.numpy as jnp
from jax import lax
from jax.experimental import pallas as pl
from jax.experimental.pallas import tpu as pltpu

D_MODEL = 1024
D_RNN = 1024
LRU_BLOCK = 64
LRU_C = 8.0
N_HEADS = 8
HEAD_DIM = 64
V_DIM = 2 * HEAD_DIM
N_BUCKETS = 32
MAX_DISTANCE = 128
N_EXPERTS = 64
TOP_K = 8
N_GROUPS = 8
GROUP_SIZE = N_EXPERTS // N_GROUPS
TOPK_GROUPS = 4
D_EXPERT = 256
ROUTED_SCALE = 2.5
EPS = 1e-6
LAM_INIT = 0.8 - 0.6 * math.exp(-0.3 * 0)

VMEM_LIMIT_BYTES = 56 * 1024 * 1024
LANES = 128
SUBLANES = 8
GATE_CHUNK = 256
MOE_EXPERTS_PER_STEP = 4
MOE_ROW_CHUNK = 256
KEY_BLOCK = 256
PIPE_SKEW = 2
S_BUFFERS = 4
V_ROWS = V_DIM + 16
LOG2E = math.log2(math.e)
WEIGHT_COLS = 128

F32 = jnp.float32
BF16 = jnp.bfloat16


def _cparams(*sem):
    return pltpu.CompilerParams(dimension_semantics=sem, vmem_limit_bytes=VMEM_LIMIT_BYTES)


def _dot(a, b):
    return jnp.dot(a, b, preferred_element_type=F32)


def _dot_nt(a, b):
    return lax.dot_general(a, b, (((1,), (1,)), ((), ())), preferred_element_type=F32)


def _split_bf16(x):
    hi = x.astype(BF16)
    lo = (x - hi.astype(F32)).astype(BF16)
    return hi, lo


def _sigmoid(x):
    return 1.0 / (1.0 + jnp.exp(-x))


def _rms(x, g):
    return x * lax.rsqrt(jnp.mean(x * x, axis=-1, keepdims=True) + EPS) * g


def _ada_kernel(c_ref, w_ref, b_ref, o_ref):
    c = c_ref[...]
    sc = c * _sigmoid(c)
    c_hi, c_lo = _split_bf16(sc)
    w_hi, w_lo = _split_bf16(w_ref[...])
    o_ref[...] = _dot(c_hi, w_hi) + _dot(c_lo, w_hi) + _dot(c_hi, w_lo) + b_ref[...]


def _ada(c_all, w_ada, b_ada, tn=1536):
    rows, d = c_all.shape
    n = w_ada.shape[1]
    return pl.pallas_call(
        _ada_kernel,
        out_shape=jax.ShapeDtypeStruct((rows, n), F32),
        grid=(n // tn,),
        in_specs=[pl.BlockSpec((rows, d), lambda j: (0, 0)),
                  pl.BlockSpec((d, tn), lambda j: (0, j)),
                  pl.BlockSpec((1, tn), lambda j: (0, j))],
        out_specs=pl.BlockSpec((rows, tn), lambda j: (0, j)),
        compiler_params=_cparams("parallel"),
        name="ada_mod",
    )(c_all, w_ada, b_ada)


def _inproj_kernel(x_ref, mod_ref, g_ref, w_ref, wvt_ref, urnn_ref, ugate_ref, q_ref, k_ref, vt_ref,
                   ga_ref, gb_ref):
    d = D_MODEL
    x = x_ref[...]
    h = _rms(x, g_ref[...]) * (1.0 + mod_ref[1:2, :]) + mod_ref[0:1, :]
    hb = h.astype(BF16)

    def col(c):
        return _dot(hb, w_ref[:, c * d:(c + 1) * d])

    urnn_ref[...] = col(0)
    ugate_ref[...] = col(1)
    q = (col(2) * (HEAD_DIM ** -0.5 * LOG2E)).astype(BF16)
    k = col(3).astype(BF16)
    vt = _dot_nt(wvt_ref[...], hb).astype(BF16)
    for hd in range(N_HEADS):
        sl = slice(hd * V_DIM, (hd + 1) * V_DIM)
        q_ref[hd] = q[:, sl]
        k_ref[hd] = k[:, sl]
        vt_ref[hd] = vt[sl, :]
    ga_ref[...] = _sigmoid(col(4))
    gb_ref[...] = _sigmoid(col(5))


def _inproj(x, mod, g, w_main_bf, w_vt_bf, tm):
    b, l, d = x.shape
    ncol = w_main_bf.shape[1]
    row = pl.BlockSpec((None, tm, d), lambda bi, i: (bi, i, 0))
    head = pl.BlockSpec((None, N_HEADS, tm, V_DIM), lambda bi, i: (bi, 0, i, 0))
    head_t = pl.BlockSpec((None, N_HEADS, V_DIM, tm), lambda bi, i: (bi, 0, 0, i))
    f32_out = jax.ShapeDtypeStruct((b, l, d), F32)
    head_out = jax.ShapeDtypeStruct((b, N_HEADS, l, V_DIM), BF16)
    head_t_out = jax.ShapeDtypeStruct((b, N_HEADS, V_DIM, l), BF16)
    return pl.pallas_call(
        _inproj_kernel,
        out_shape=(f32_out, f32_out, head_out, head_out, head_t_out, f32_out, f32_out),
        grid=(b, l // tm),
        in_specs=[row,
                  pl.BlockSpec((None, 6, d), lambda bi, i: (bi, 0, 0)),
                  pl.BlockSpec((1, d), lambda bi, i: (0, 0)),
                  pl.BlockSpec((d, ncol), lambda bi, i: (0, 0), pipeline_mode=pl.Buffered(1)),
                  pl.BlockSpec((d, d), lambda bi, i: (0, 0), pipeline_mode=pl.Buffered(1))],
        out_specs=(row, row, head, head, head_t, row, row),
        compiler_params=_cparams("parallel", "parallel"),
        name="in_proj",
    )(x, mod, g, w_main_bf, w_vt_bf)


def _gelu_tanh(x):
    return x * (0.5 * (1.0 + jnp.tanh(math.sqrt(2.0 / math.pi) * (x + 0.044715 * (x * x * x)))))


def _scan_kernel(*refs, tl, nt, reverse):
    if reverse:
        (u_ref, prev_ref, next_ref, cw_ref, cb_ref, wbd_ref, br_ref, bi_ref, lam_ref,
         hf_ref, ug_ref, o_ref, carry_ref, h_ref) = refs
    else:
        (u_ref, prev_ref, next_ref, cw_ref, cb_ref, wbd_ref, br_ref, bi_ref, lam_ref,
         o_ref, carry_ref, h_ref) = refs
    c = D_RNN
    step = pl.program_id(1)
    t = (nt - 1 - step) if reverse else step

    @pl.when(step == 0)
    def _():
        carry_ref[...] = jnp.zeros_like(carry_ref)

    u = u_ref[...]
    row = lax.broadcasted_iota(jnp.int32, (tl, c), 0)
    halo_lo = jnp.where(t > 0, prev_ref[...], 0.0)
    halo_hi = jnp.where(t < nt - 1, next_ref[...], 0.0)
    xc = cb_ref[...] + cw_ref[2:3, :] * u
    for k in (2, 1):
        lo_fill = jnp.concatenate([pltpu.roll(halo_lo, k, 0)] + [halo_lo] * (tl // SUBLANES - 1), axis=0)
        shifted = jnp.where(row < k, lo_fill, pltpu.roll(u, k, 0))
        xc = xc + cw_ref[2 - k:3 - k, :] * shifted
    hi_fill = jnp.concatenate([halo_hi] * (tl // SUBLANES - 1) + [pltpu.roll(halo_hi, SUBLANES - 1, 0)],
                              axis=0)
    shifted = jnp.where(row >= tl - 1, hi_fill, pltpu.roll(u, tl - 1, 0))
    xc = xc + cw_ref[3:4, :] * shifted

    xcb = xc.astype(BF16)
    r_parts, i_parts = [], []
    for ch in range(c // GATE_CHUNK):
        z = _dot(xcb[:, ch * GATE_CHUNK:(ch + 1) * GATE_CHUNK], wbd_ref[ch])
        r_parts.append(z[:, :GATE_CHUNK])
        i_parts.append(z[:, GATE_CHUNK:])
    r = _sigmoid(jnp.concatenate(r_parts, axis=-1) + br_ref[...])
    ig = _sigmoid(jnp.concatenate(i_parts, axis=-1) + bi_ref[...])
    nl = -lam_ref[...]
    softplus = jnp.maximum(nl, 0.0) + jnp.log1p(jnp.exp(-jnp.abs(nl)))
    log_a = (-LRU_C) * r * softplus
    a = jnp.exp(log_a)
    one_m_a2 = -jnp.tanh(log_a) * (1.0 + a * a)
    root = jnp.where(one_m_a2 > 0.0, one_m_a2 * lax.rsqrt(one_m_a2), 0.0)
    bb = root * (ig * xc)

    ngroups = tl // SUBLANES
    a = a.reshape(ngroups, SUBLANES, c)
    bb = bb.reshape(ngroups, SUBLANES, c)
    pos = lax.broadcasted_iota(jnp.int32, (ngroups, SUBLANES, c), 1)
    s = 1
    while s < SUBLANES:
        if reverse:
            valid = pos < SUBLANES - s
            shift = SUBLANES - s
        else:
            valid = pos >= s
            shift = s
        a_n = jnp.where(valid, pltpu.roll(a, shift, 1), 1.0)
        b_n = jnp.where(valid, pltpu.roll(bb, shift, 1), 0.0)
        bb = a * b_n + bb
        a = a * a_n
        s *= 2
    carry = carry_ref[...]
    for step_g in range(ngroups):
        g = (ngroups - 1 - step_g) if reverse else step_g
        h_g = a[g] * carry + bb[g]
        h_ref[g * SUBLANES:(g + 1) * SUBLANES, :] = h_g
        carry = h_g[0:1, :] if reverse else h_g[SUBLANES - 1:SUBLANES, :]
    carry_ref[...] = carry

    if reverse:
        o_ref[...] = (_gelu_tanh(ug_ref[...]) * (hf_ref[...] + h_ref[...])).astype(o_ref.dtype)
    else:
        o_ref[...] = h_ref[...]


def _scan(u, conv_w, conv_b, wbd, b_r, b_i, lam, tl, h_fwd=None, u_gate=None):
    b, l, c = u.shape
    nt = l // tl
    reverse = h_fwd is not None
    tpb = tl // SUBLANES
    nblk = l // SUBLANES

    def tix(s):
        return (nt - 1 - s) if reverse else s

    row = pl.BlockSpec((None, tl, c), lambda bi, s: (bi, tix(s), 0))
    prev = pl.BlockSpec((None, SUBLANES, c), lambda bi, s: (bi, jnp.maximum(tix(s) * tpb - 1, 0), 0))
    nxt = pl.BlockSpec((None, SUBLANES, c),
                       lambda bi, s: (bi, jnp.minimum((tix(s) + 1) * tpb, nblk - 1), 0))
    vec = pl.BlockSpec((1, c), lambda bi, s: (0, 0))
    in_specs = [row, prev, nxt,
                pl.BlockSpec((4, c), lambda bi, s: (0, 0)), vec,
                pl.BlockSpec(wbd.shape, lambda bi, s: (0, 0, 0)), vec, vec, vec]
    args = [u, u, u, conv_w, conv_b, wbd, b_r, b_i, lam]
    if reverse:
        in_specs += [row, row]
        args += [h_fwd, u_gate]
        out_dtype = BF16
    else:
        out_dtype = F32
    return pl.pallas_call(
        functools.partial(_scan_kernel, tl=tl, nt=nt, reverse=reverse),
        out_shape=jax.ShapeDtypeStruct((b, l, c), out_dtype),
        grid=(b, nt),
        in_specs=in_specs,
        out_specs=row,
        scratch_shapes=[pltpu.VMEM((1, c), F32), pltpu.VMEM((tl, c), F32)],
        compiler_params=_cparams("parallel", "arbitrary"),
        name="rglru_bwd" if reverse else "rglru_fwd",
    )(*args)


def _bias_kernel(z_ref, o_ref, *, t, rc):
    y = jnp.broadcast_to(z_ref[...], (LANES, 2 * t))
    row = lax.broadcasted_iota(jnp.int32, (LANES, 2 * t), 0)
    bit = 0
    while (1 << bit) < LANES:
        y = jnp.where(((row >> bit) & 1) == 1, pltpu.roll(y, 1 << bit, 1), y)
        bit += 1
    for rh in range(t // LANES):
        blk = pltpu.roll(y, LANES * rh, 1) if rh else y
        for c in range(t // rc):
            o_ref[c, rh * LANES:(rh + 1) * LANES, :] = blk[:, c * rc:(c + 1) * rc]


def _bias_tiles(z, t, rc):
    nh = z.shape[0]
    return pl.pallas_call(
        functools.partial(_bias_kernel, t=t, rc=rc),
        out_shape=jax.ShapeDtypeStruct((nh, 3, t // rc, t, rc), F32),
        grid=(nh, 3),
        in_specs=[pl.BlockSpec((None, None, 1, 2 * t), lambda h, o: (h, o, 0, 0))],
        out_specs=pl.BlockSpec((None, None, t // rc, t, rc), lambda h, o: (h, o, 0, 0, 0)),
        compiler_params=_cparams("parallel", "parallel"),
        name="bias_tiles",
    )(z)


def _reduce_row_groups(s, op):
    parts = [s[r:r + SUBLANES, :] for r in range(0, s.shape[0], SUBLANES)]
    while len(parts) > 1:
        parts = [op(a, b) for a, b in zip(parts[0::2], parts[1::2])] + \
                ([parts[-1]] if len(parts) % 2 else [])
    return parts[0]


def _attn_kernel(far_ref, q_ref, k_ref, vt_ref, bias_ref, lamp_ref, g_ref, o_ref,
                 qz_ref, vta_ref, m_ref, alpha_ref, acc_ref, s_ref, *, nk, rc):
    hd = pl.program_id(1)
    i = pl.program_id(2)
    j = pl.program_id(3)
    t = q_ref.shape[0]
    nc = t // rc

    nunits = 2 * nc
    nkb = t // KEY_BLOCK
    assert nunits % S_BUFFERS == 0 and PIPE_SKEW < S_BUFFERS
    cur = j % 2
    prev = 1 - cur

    @pl.when(j == 0)
    def _():
        q = q_ref[...]
        lane = lax.broadcasted_iota(jnp.int32, q.shape, 1)
        qz_ref[0] = jnp.where(lane < HEAD_DIM, q, jnp.zeros_like(q))
        qz_ref[1] = jnp.where(lane >= HEAD_DIM, q, jnp.zeros_like(q))
        m_ref[...] = jnp.full_like(m_ref, -jnp.inf)
        alpha_ref[...] = jnp.ones_like(alpha_ref)
        acc_ref[...] = jnp.zeros_like(acc_ref)
        for u in range(nunits - PIPE_SKEW, nunits):
            s_ref[u % S_BUFFERS] = jnp.zeros(s_ref.shape[1:], F32)
        ones_row = lax.broadcasted_iota(jnp.int32, (V_ROWS - V_DIM, t), 0) == 0
        for side in range(2):
            vta_ref[side, 0:V_DIM, :] = jnp.zeros((V_DIM, t), BF16)
            vta_ref[side, V_DIM:, :] = jnp.where(ones_row, 1.0, 0.0).astype(BF16)

    vta_ref[cur, 0:V_DIM, :] = vt_ref[...]

    shift = (jnp.where(j == jnp.maximum(i - 1, 0), far_ref[0, hd], 0.0)
             - jnp.where(j == jnp.minimum(i + 1, nk - 1) + 1, far_ref[1, hd], 0.0))

    def value_stage_begin(u, carried):
        c, br = divmod(u, 2)
        m_u = m_ref[br, c]
        if carried:
            m_u = jnp.where(j > 0, m_u, jnp.inf)
        return m_u

    def value_stage_block(u, m_u, side, ks, accp):
        p = jnp.exp2(s_ref[u % S_BUFFERS, ks, :] - m_u)
        pv = _dot(vta_ref[side, :, ks], p.astype(BF16))
        return pv if accp is None else accp + pv

    def value_stage_end(u, accp):
        c, br = divmod(u, 2)
        acc_ref[br, c] = alpha_ref[br, c] * acc_ref[br, c] + accp

    def step(with_bias):
        for slot in range(nunits):
            ca, bra = divmod(slot, 2)
            qa = qz_ref[bra, ca * rc:(ca + 1) * rc, :]
            colmax = None
            carried = slot < PIPE_SKEW
            ub = slot - PIPE_SKEW + (nunits if carried else 0)
            side = prev if carried else cur
            m_b = value_stage_begin(ub, carried)
            accp = None
            for kb in range(nkb):
                ks = slice(kb * KEY_BLOCK, (kb + 1) * KEY_BLOCK)
                s = _dot_nt(k_ref[ks, :], qa)
                if with_bias:
                    s = s + bias_ref[ca, ks, :]
                s_ref[slot % S_BUFFERS, ks, :] = s
                bmax = _reduce_row_groups(s, jnp.maximum)
                colmax = bmax if colmax is None else jnp.maximum(colmax, bmax)
                accp = value_stage_block(ub, m_b, side, ks, accp)
            m_old = m_ref[bra, ca] + shift
            m_new = jnp.maximum(m_old, jnp.max(colmax, axis=0, keepdims=True))
            alpha_ref[bra, ca] = jnp.exp2(m_old - m_new)
            m_ref[bra, ca] = m_new
            value_stage_end(ub, accp)

    is_band = jnp.abs(j - i) <= 1

    @pl.when(is_band)
    def _():
        step(True)

    @pl.when(jnp.logical_not(is_band))
    def _():
        step(False)

    @pl.when(j == nk - 1)
    def _():
        for u in range(nunits - PIPE_SKEW, nunits):
            m_u = value_stage_begin(u, False)
            accp = None
            for kb in range(nkb):
                accp = value_stage_block(u, m_u, cur, slice(kb * KEY_BLOCK, (kb + 1) * KEY_BLOCK), accp)
            value_stage_end(u, accp)
        lp = lamp_ref[...]
        lam = (jnp.exp(jnp.sum(lp[0:1, :] * lp[1:2, :], axis=-1, keepdims=True))
               - jnp.exp(jnp.sum(lp[2:3, :] * lp[3:4, :], axis=-1, keepdims=True)) + LAM_INIT)
        for c in range(nc):
            o = (acc_ref[0, c, 0:V_DIM, :] / acc_ref[0, c, V_DIM:V_DIM + 1, :]
                 - lam * (acc_ref[1, c, 0:V_DIM, :] / acc_ref[1, c, V_DIM:V_DIM + 1, :]))
            o = o * lax.rsqrt(jnp.mean(o * o, axis=0, keepdims=True) + EPS) * g_ref[...]
            o = o * (1.0 - LAM_INIT)
            o_ref[c * rc:(c + 1) * rc, :] = o.T.astype(o_ref.dtype)


def _attention(q, k, vt, bias_tiles, far_bias, lam_params, subln_g_col, t, rc):
    b, nh, l, dv = q.shape
    nq = l // t
    nc = t // rc
    qspec = pl.BlockSpec((None, None, t, dv), lambda bi, h, i, j: (bi, h, i, 0))
    kspec = pl.BlockSpec((None, None, t, dv), lambda bi, h, i, j: (bi, h, j, 0))
    vspec = pl.BlockSpec((None, None, dv, t), lambda bi, h, i, j: (bi, h, 0, j))
    return pl.pallas_call(
        functools.partial(_attn_kernel, nk=nq, rc=rc),
        out_shape=jax.ShapeDtypeStruct((b, nh, l, dv), BF16),
        grid=(b, nh, nq, nq),
        in_specs=[pl.BlockSpec(memory_space=pltpu.SMEM),
                  qspec, kspec, vspec,
                  pl.BlockSpec((None, None, nc, t, rc),
                               lambda bi, h, i, j: (h, jnp.clip(j - i, -1, 1) + 1, 0, 0, 0)),
                  pl.BlockSpec((4, HEAD_DIM), lambda bi, h, i, j: (0, 0)),
                  pl.BlockSpec((dv, 1), lambda bi, h, i, j: (0, 0))],
        out_specs=qspec,
        scratch_shapes=[pltpu.VMEM((2, t, dv), BF16), pltpu.VMEM((2, V_ROWS, t), BF16),
                        pltpu.VMEM((2, nc, 1, rc), F32), pltpu.VMEM((2, nc, 1, rc), F32),
                        pltpu.VMEM((2, nc, V_ROWS, rc), F32),
                        pltpu.VMEM((S_BUFFERS, t, rc), F32)],
        compiler_params=_cparams("parallel", "parallel", "parallel", "arbitrary"),
        name="diff_attn",
    )(far_bias, q, k, vt, bias_tiles, lam_params, subln_g_col)


def _merge_kernel(ga_ref, o_ref, gma_ref, gmb_ref, x_ref, mod_ref, g_ref, wrnn_ref, wattn_ref,
                  wo_ref, wrh_ref, wrl_ref, x1_ref, h2_ref, s_ref):
    ya = _dot(ga_ref[...], wrnn_ref[...])
    ob = jnp.concatenate([o_ref[hd] for hd in range(N_HEADS)], axis=-1)
    yb = _dot(ob, wattn_ref[...])
    mixed = (gma_ref[...] * ya + gmb_ref[...] * yb).astype(BF16)
    x1 = x_ref[...] + mod_ref[2:3, :] * _dot(mixed, wo_ref[...])
    x1_ref[...] = x1
    h2 = _rms(x1, g_ref[...]) * (1.0 + mod_ref[4:5, :]) + mod_ref[3:4, :]
    h2_hi, h2_lo = _split_bf16(h2)
    h2_ref[...] = h2_hi
    logits = (_dot_nt(wrh_ref[...], h2_hi) + _dot_nt(wrh_ref[...], h2_lo)
              + _dot_nt(wrl_ref[...], h2_hi))
    s_ref[...] = _sigmoid(logits)


def _merge(ga, o, gm_a, gm_b, x, mod, g, w_rnn_bf, w_attn_bf, w_o_bf, wr_hi, wr_lo, tm):
    b, l, d = x.shape
    row = pl.BlockSpec((None, tm, d), lambda bi, i: (bi, i, 0))
    wspec = pl.BlockSpec((d, d), lambda bi, i: (0, 0), pipeline_mode=pl.Buffered(1))
    rspec = pl.BlockSpec((N_EXPERTS, d), lambda bi, i: (0, 0))
    return pl.pallas_call(
        _merge_kernel,
        out_shape=(jax.ShapeDtypeStruct((b, l, d), F32),
                   jax.ShapeDtypeStruct((b, l, d), BF16),
                   jax.ShapeDtypeStruct((b, N_EXPERTS, l), F32)),
        grid=(b, l // tm),
        in_specs=[row,
                  pl.BlockSpec((None, N_HEADS, tm, V_DIM), lambda bi, i: (bi, 0, i, 0)),
                  row, row, row,
                  pl.BlockSpec((None, 6, d), lambda bi, i: (bi, 0, 0)),
                  pl.BlockSpec((1, d), lambda bi, i: (0, 0)),
                  wspec, wspec, wspec, rspec, rspec],
        out_specs=(row, row, pl.BlockSpec((None, N_EXPERTS, tm), lambda bi, i: (bi, 0, i))),
        compiler_params=_cparams("parallel", "parallel"),
        name="merge_router",
    )(ga, o, gm_a, gm_b, x, mod, g, w_rnn_bf, w_attn_bf, w_o_bf, wr_hi, wr_lo)


def _route_kernel(s_ref, rb_ref, w_ref):
    tr = s_ref.shape[1]
    neg = -jnp.inf
    sub = lax.broadcasted_iota(jnp.int32, (GROUP_SIZE, tr), 0)
    s_g = [s_ref[g * GROUP_SIZE:(g + 1) * GROUP_SIZE, :] for g in range(N_GROUPS)]
    sb_g = [s_g[g] + rb_ref[g * GROUP_SIZE:(g + 1) * GROUP_SIZE, :] for g in range(N_GROUPS)]

    gscore = []
    for g in range(N_GROUPS):
        xg = sb_g[g]
        top1 = jnp.max(xg, axis=0, keepdims=True)
        first = jnp.min(jnp.where(xg == top1, sub, GROUP_SIZE), axis=0, keepdims=True)
        top2 = jnp.max(jnp.where(sub == first, neg, xg), axis=0, keepdims=True)
        gscore.append(top1 + top2)

    masked = []
    for g in range(N_GROUPS):
        rank = jnp.zeros((1, tr), jnp.int32)
        for g2 in range(N_GROUPS):
            if g2 == g:
                continue
            ahead = (gscore[g2] >= gscore[g]) if g2 < g else (gscore[g2] > gscore[g])
            rank = rank + ahead.astype(jnp.int32)
        keep = jnp.broadcast_to(rank < TOPK_GROUPS, (GROUP_SIZE, tr))
        masked.append(jnp.where(keep, sb_g[g], neg))

    ranks = [jnp.zeros((GROUP_SIZE, tr), jnp.int32) for _ in range(N_GROUPS)]
    for g2 in range(N_GROUPS):
        for r2 in range(GROUP_SIZE):
            other = jnp.broadcast_to(masked[g2][r2:r2 + 1, :], (GROUP_SIZE, tr))
            for g in range(N_GROUPS):
                mine = masked[g]
                if g2 < g:
                    ahead = other >= mine
                elif g2 > g:
                    ahead = other > mine
                else:
                    ahead = (other > mine) | ((other == mine) & (sub > r2))
                ranks[g] = ranks[g] + ahead.astype(jnp.int32)

    picked = [jnp.where(ranks[g] < TOP_K, s_g[g], 0.0) for g in range(N_GROUPS)]
    total = picked[0]
    for g in range(1, N_GROUPS):
        total = total + picked[g]
    denom = jnp.sum(total, axis=0, keepdims=True)
    for g in range(N_GROUPS):
        w_ref[g * GROUP_SIZE:(g + 1) * GROUP_SIZE, :] = picked[g] / denom * ROUTED_SCALE
    w_ref[N_EXPERTS:N_EXPERTS + GROUP_SIZE, :] = jnp.where(sub == 0, 1.0, 0.0)
    w_ref[N_EXPERTS + GROUP_SIZE:, :] = jnp.zeros((WEIGHT_COLS - N_EXPERTS - GROUP_SIZE, tr), F32)


def _route(s_t, router_bias_col, tr):
    b, ne, l = s_t.shape
    return pl.pallas_call(
        _route_kernel,
        out_shape=jax.ShapeDtypeStruct((b, WEIGHT_COLS, l), F32),
        grid=(b, l // tr),
        in_specs=[pl.BlockSpec((None, ne, tr), lambda bi, i: (bi, 0, i)),
                  pl.BlockSpec((ne, 1), lambda bi, i: (0, 0))],
        out_specs=pl.BlockSpec((None, WEIGHT_COLS, tr), lambda bi, i: (bi, 0, i)),
        compiler_params=_cparams("parallel", "parallel"),
        name="route_topk",
    )(s_t, router_bias_col)


def _moe_kernel(h_ref, w_ref, x1_ref, mod_ref, g_ref, w1_ref, w3_ref, w2_ref, w2p_ref,
                w1s_ref, w3s_ref, w2s_ref, o_ref, acc_ref, hid_ref, *, nsteps, rc):
    st = pl.program_id(2)
    tm = h_ref.shape[0]
    eps = MOE_EXPERTS_PER_STEP
    chunks = [slice(r * rc, (r + 1) * rc) for r in range(tm // rc)]

    @pl.when(st == 0)
    def _():
        acc_ref[...] = jnp.zeros_like(acc_ref)
        hid_ref[...] = jnp.zeros_like(hid_ref)

    w = w_ref[...]
    lane = lax.broadcasted_iota(jnp.int32, w.shape, 1)

    def up_chunk(slot, w1, w3, wcol, rows):
        h = h_ref[rows, :]
        a1 = _dot(h, w1)
        hidden = (a1 * _sigmoid(a1)) * _dot(h, w3)
        if wcol is not None:
            hidden = hidden * wcol[rows, :]
        hid_ref[slot % 2, rows, :] = hidden.astype(BF16)

    def down_chunk(slot, w2, rows):
        acc_ref[rows, :] += _dot(hid_ref[slot % 2, rows, :], w2)

    for s in range(eps):
        wcol = jnp.sum(jnp.where(lane == st * eps + s, w, 0.0), axis=-1, keepdims=True)
        w2_prev = w2p_ref[...] if s == 0 else w2_ref[s - 1]
        for rows in chunks:
            up_chunk(s, w1_ref[s], w3_ref[s], wcol, rows)
            down_chunk(s - 1, w2_prev, rows)

    @pl.when(st == nsteps - 1)
    def _():
        for rows in chunks:
            up_chunk(eps, w1s_ref[...], w3s_ref[...], None, rows)
            down_chunk(eps - 1, w2_ref[eps - 1], rows)
        for rows in chunks:
            down_chunk(eps, w2s_ref[...], rows)
        y = x1_ref[...] + mod_ref[5:6, :] * acc_ref[...]
        o_ref[...] = _rms(y, g_ref[...])


def _moe(h2, w_tok, x1, mod, final_g, w1_all, w3_all, w2_all, w1_s, w3_s, w2_s, tm, rc):
    b, l, d = x1.shape
    eps = MOE_EXPERTS_PER_STEP
    assert eps % 2 == 0 and w1_all.shape[0] % eps == 0
    nsteps = w1_all.shape[0] // eps
    row = pl.BlockSpec((None, tm, d), lambda bi, i, e: (bi, i, 0))
    up_s = pl.BlockSpec((d, D_EXPERT), lambda bi, i, e: (0, 0))
    return pl.pallas_call(
        functools.partial(_moe_kernel, nsteps=nsteps, rc=rc),
        out_shape=jax.ShapeDtypeStruct((b, l, d), F32),
        grid=(b, l // tm, nsteps),
        in_specs=[pl.BlockSpec((None, tm, d), lambda bi, i, e: (bi, i, 0)),
                  pl.BlockSpec((None, tm, WEIGHT_COLS), lambda bi, i, e: (bi, i, 0)),
                  row,
                  pl.BlockSpec((None, 6, d), lambda bi, i, e: (bi, 0, 0)),
                  pl.BlockSpec((1, d), lambda bi, i, e: (0, 0)),
                  pl.BlockSpec((eps, d, D_EXPERT), lambda bi, i, e: (e, 0, 0)),
                  pl.BlockSpec((eps, d, D_EXPERT), lambda bi, i, e: (e, 0, 0)),
                  pl.BlockSpec((eps, D_EXPERT, d), lambda bi, i, e: (e, 0, 0)),
                  pl.BlockSpec((None, D_EXPERT, d), lambda bi, i, e: (jnp.maximum(e * eps - 1, 0), 0, 0)),
                  up_s, up_s, pl.BlockSpec((D_EXPERT, d), lambda bi, i, e: (0, 0))],
        out_specs=row,
        scratch_shapes=[pltpu.VMEM((tm, d), F32), pltpu.VMEM((2, tm, D_EXPERT), BF16)],
        compiler_params=_cparams("parallel", "parallel", "arbitrary"),
        name="moe_experts",
    )(h2, w_tok, x1, mod, final_g, w1_all, w3_all, w2_all, w2_all, w1_s, w3_s, w2_s)


def _t5_bucket_np(rel):
    half = N_BUCKETS // 2
    max_exact = half // 2
    ret = np.where(rel > 0, half, 0)
    n = np.abs(rel)
    nf = np.maximum(n, 1).astype(np.float32)
    large = max_exact + (np.log(nf / np.float32(max_exact)) / np.float32(math.log(MAX_DISTANCE / max_exact))
                         * np.float32(half - max_exact)).astype(np.int32)
    large = np.minimum(large, half - 1)
    return (ret + np.where(n < max_exact, n, large)).astype(np.int32)


def _bias_tables(rel_bias, t):
    assert t >= MAX_DISTANCE
    v = np.arange(2 * t)
    d = np.where(v < t, v, v - 2 * t)
    rel = np.stack([off - d for off in (-t, 0, t)])
    scaled = rel_bias.astype(F32) * LOG2E
    z = jnp.take(scaled, jnp.asarray(_t5_bucket_np(rel)), axis=0)
    z = jnp.transpose(z, (2, 0, 1))[:, :, None, :]
    far = jnp.stack([scaled[int(_t5_bucket_np(np.array(-MAX_DISTANCE)))],
                     scaled[int(_t5_bucket_np(np.array(MAX_DISTANCE)))]])
    return z, far


def _block_diag_gates(w_r, w_i):
    per = GATE_CHUNK // LRU_BLOCK
    eye = jnp.eye(per, dtype=F32)

    def bd(w):
        w = w.reshape(-1, per, LRU_BLOCK, LRU_BLOCK)
        return jnp.einsum('cjab,jk->cjakb', w, eye).reshape(-1, GATE_CHUNK, GATE_CHUNK)

    return jnp.concatenate([bd(w_r), bd(w_i)], axis=-1).astype(BF16)


def _tile(l, pref):
    return min(l, pref)


def _trunk(x, mod, p, tiles=None):
    b, l, d = x.shape
    tl = dict(inproj=512, scan=256, attn=1024, attn_rows=256, merge=512, route=512, moe=1024)
    if tiles:
        tl.update(tiles)
    tl = {k: _tile(l, v) for k, v in tl.items()}
    tl['attn_rows'] = min(tl['attn_rows'], tl['attn'])

    u_rnn, u_gate, q, k, vt, gm_a, gm_b = _inproj(x, mod, p['norm_mix_g'], p['w_main'], p['w_vt'],
                                                  tl['inproj'])

    h_fwd = _scan(u_rnn, p['conv_w'], p['conv_b'], p['wbd'][0], p['b_r'][0:1], p['b_i'][0:1],
                  p['lam'][0:1], tl['scan'])
    ga = _scan(u_rnn, p['conv_w'], p['conv_b'], p['wbd'][1], p['b_r'][1:2], p['b_i'][1:2],
               p['lam'][1:2], tl['scan'], h_fwd=h_fwd, u_gate=u_gate)

    bias_tiles, far = p['bias'](tl['attn'], tl['attn_rows'])
    o = _attention(q, k, vt, bias_tiles, far, p['lam_params'], p['subln_g'], tl['attn'], tl['attn_rows'])

    x1, h2, s_t = _merge(ga, o, gm_a, gm_b, x, mod, p['norm_ffn_g'], p['w_rnn_out'], p['w_attn_out'],
                         p['w_o'], p['wr_hi'], p['wr_lo'], tl['merge'])
    w_t = _route(s_t, p['router_bias'], tl['route'])
    w_tok = jnp.swapaxes(w_t, 1, 2)
    return _moe(h2, w_tok, x1, mod, p['final_norm_g'], p['w1'], p['w3'], p['w2'],
                p['w1_s'], p['w3_s'], p['w2_s'], tl['moe'], min(tl['moe'], MOE_ROW_CHUNK))


def _prepare(norm_mix_g, norm_ffn_g, final_norm_g, w_in, conv_w, conv_b, w_rgate, b_rgate, w_igate,
             b_igate, lru_lambda, w_rnn_out, lambda_q1, lambda_k1, lambda_q2, lambda_k2, subln_g,
             rel_bias, w_attn_out, w_o, w_router, router_bias, w1_e, w3_e, w2_e, w1_s, w3_s, w2_s):
    wr_t = w_router[0].T.astype(F32)
    wr_hi = wr_t.astype(BF16)
    wr_lo = (wr_t - wr_hi.astype(F32)).astype(BF16)
    bias_cache = {}

    def bias(t, rc):
        if (t, rc) not in bias_cache:
            z, far = _bias_tables(rel_bias, t)
            bias_cache[(t, rc)] = (_bias_tiles(z, t, rc), far)
        return bias_cache[(t, rc)]

    w_in0 = w_in[0]
    v_lo = 2 * D_RNN + 2 * N_HEADS * 2 * HEAD_DIM
    v_hi = v_lo + N_HEADS * V_DIM
    return dict(
        norm_mix_g=norm_mix_g[0][None], norm_ffn_g=norm_ffn_g[0][None], final_norm_g=final_norm_g[None],
        w_main=jnp.concatenate([w_in0[:, :v_lo], w_in0[:, v_hi:]], axis=1).astype(BF16),
        w_vt=w_in0[:, v_lo:v_hi].T.astype(BF16),
        conv_w=conv_w[0], conv_b=conv_b[0][None],
        wbd=jnp.stack([_block_diag_gates(w_rgate[0, dr], w_igate[0, dr]) for dr in range(2)]),
        b_r=b_rgate[0], b_i=b_igate[0], lam=lru_lambda[0],
        w_rnn_out=w_rnn_out[0].astype(BF16), w_attn_out=w_attn_out[0].astype(BF16),
        w_o=w_o[0].astype(BF16),
        lam_params=jnp.stack([lambda_q1[0], lambda_k1[0], lambda_q2[0], lambda_k2[0]]).astype(F32),
        subln_g=subln_g[0][:, None].astype(F32), bias=bias,
        wr_hi=wr_hi, wr_lo=wr_lo, router_bias=router_bias[0][:, None].astype(F32),
        w1=w1_e[0].astype(BF16), w3=w3_e[0].astype(BF16), w2=w2_e[0].astype(BF16),
        w1_s=w1_s[0].astype(BF16), w3_s=w3_s[0].astype(BF16), w2_s=w2_s[0].astype(BF16),
    )


def kernel(x_prompt, x_sample, c_prompt, c_sample, norm_mix_g, norm_ffn_g, final_norm_g, w_ada, b_ada, w_in, conv_w, conv_b, w_rgate, b_rgate, w_igate, b_igate, lru_lambda, w_rnn_out, lambda_q1, lambda_k1, lambda_q2, lambda_k2, subln_g, rel_bias, w_attn_out, w_o, w_router, router_bias, w1_e, w3_e, w2_e, w1_s, w3_s, w2_s):
    d = D_MODEL
    nb_p, nb_s = c_prompt.shape[0], c_sample.shape[0]
    pad = (-(nb_p + nb_s)) % SUBLANES
    c_all = jnp.concatenate([c_prompt, c_sample, jnp.zeros((pad, d), F32)], axis=0)
    mod = _ada(c_all, w_ada[0], b_ada[0][None]).reshape(-1, 6, d)
    p = _prepare(norm_mix_g, norm_ffn_g, final_norm_g, w_in, conv_w, conv_b, w_rgate, b_rgate,
                 w_igate, b_igate, lru_lambda, w_rnn_out, lambda_q1, lambda_k1, lambda_q2, lambda_k2,
                 subln_g, rel_bias, w_attn_out, w_o, w_router, router_bias, w1_e, w3_e, w2_e,
                 w1_s, w3_s, w2_s)
    y_prompt = _trunk(x_prompt, mod[:nb_p], p)
    y_sample = _trunk(x_sample, mod[nb_p:nb_p + nb_s], p)
    return (y_prompt, y_sample)
```

```python
import functools
import math

import numpy as np
import jax
import jax.numpy as jnp
from jax import lax
from jax.experimental import pallas as pl
from jax.experimental.pallas import tpu as pltpu

D_MODEL = 1024
D_RNN = 1024
LRU_BLOCK = 64
LRU_C = 8.0
N_HEADS = 8
HEAD_DIM = 64
V_DIM = 2 * HEAD_DIM
N_BUCKETS = 32
MAX_DISTANCE = 128
N_EXPERTS = 64
TOP_K = 8
N_GROUPS = 8
GROUP_SIZE = N_EXPERTS // N_GROUPS
TOPK_GROUPS = 4
D_EXPERT = 256
ROUTED_SCALE = 2.5
EPS = 1e-6
LAM_INIT = 0.8 - 0.6 * math.exp(-0.3 * 0)

VMEM_LIMIT_BYTES = 56 * 1024 * 1024
LANES = 128
SUBLANES = 8
GATE_CHUNK = 256
MOE_EXPERTS_PER_STEP = 4
MOE_ROW_CHUNK = 256
KEY_BLOCK = 256
ATTN_HEADS_PER_STEP = 4
PIPE_SKEW = 2
S_BUFFERS = 4
V_ROWS = V_DIM + 16
LOG2E = math.log2(math.e)
WEIGHT_COLS = 128

F32 = jnp.float32
BF16 = jnp.bfloat16


def _cparams(*sem):
    return pltpu.CompilerParams(dimension_semantics=sem, vmem_limit_bytes=VMEM_LIMIT_BYTES)


def _dot(a, b):
    return jnp.dot(a, b, preferred_element_type=F32)


def _dot_nt(a, b):
    return lax.dot_general(a, b, (((1,), (1,)), ((), ())), preferred_element_type=F32)


def _split_bf16(x):
    hi = x.astype(BF16)
    lo = (x - hi.astype(F32)).astype(BF16)
    return hi, lo


def _sigmoid(x):
    return 1.0 / (1.0 + jnp.exp(-x))


def _rms(x, g):
    return x * lax.rsqrt(jnp.mean(x * x, axis=-1, keepdims=True) + EPS) * g


def _ada_kernel(c_ref, w_ref, b_ref, o_ref):
    c = c_ref[...]
    sc = c * _sigmoid(c)
    c_hi, c_lo = _split_bf16(sc)
    w_hi, w_lo = _split_bf16(w_ref[...])
    o_ref[...] = _dot(c_hi, w_hi) + _dot(c_lo, w_hi) + _dot(c_hi, w_lo) + b_ref[...]


def _ada(c_all, w_ada, b_ada, tn=1536):
    rows, d = c_all.shape
    n = w_ada.shape[1]
    return pl.pallas_call(
        _ada_kernel,
        out_shape=jax.ShapeDtypeStruct((rows, n), F32),
        grid=(n // tn,),
        in_specs=[pl.BlockSpec((rows, d), lambda j: (0, 0)),
                  pl.BlockSpec((d, tn), lambda j: (0, j)),
                  pl.BlockSpec((1, tn), lambda j: (0, j))],
        out_specs=pl.BlockSpec((rows, tn), lambda j: (0, j)),
        compiler_params=_cparams("parallel"),
        name="ada_mod",
    )(c_all, w_ada, b_ada)


def _inproj_kernel(x_ref, mod_ref, g_ref, w_ref, wvt_ref, urnn_ref, ugate_ref, q_ref, k_ref, vt_ref,
                   ga_ref, gb_ref):
    d = D_MODEL
    x = x_ref[...]
    h = _rms(x, g_ref[...]) * (1.0 + mod_ref[1:2, :]) + mod_ref[0:1, :]
    hb = h.astype(BF16)

    def col(c):
        return _dot(hb, w_ref[:, c * d:(c + 1) * d])

    urnn_ref[...] = col(0)
    ugate_ref[...] = col(1)
    q = (col(2) * (HEAD_DIM ** -0.5 * LOG2E)).astype(BF16)
    k = col(3).astype(BF16)
    vt = _dot_nt(wvt_ref[...], hb).astype(BF16)
    for hd in range(N_HEADS):
        sl = slice(hd * V_DIM, (hd + 1) * V_DIM)
        q_ref[hd] = q[:, sl]
        k_ref[hd] = k[:, sl]
        vt_ref[hd] = vt[sl, :]
    ga_ref[...] = _sigmoid(col(4))
    gb_ref[...] = _sigmoid(col(5))


def _inproj(x, mod, g, w_main_bf, w_vt_bf, tm):
    b, l, d = x.shape
    ncol = w_main_bf.shape[1]
    row = pl.BlockSpec((None, tm, d), lambda bi, i: (bi, i, 0))
    head = pl.BlockSpec((None, N_HEADS, tm, V_DIM), lambda bi, i: (bi, 0, i, 0))
    head_t = pl.BlockSpec((None, N_HEADS, V_DIM, tm), lambda bi, i: (bi, 0, 0, i))
    f32_out = jax.ShapeDtypeStruct((b, l, d), F32)
    head_out = jax.ShapeDtypeStruct((b, N_HEADS, l, V_DIM), BF16)
    head_t_out = jax.ShapeDtypeStruct((b, N_HEADS, V_DIM, l), BF16)
    return pl.pallas_call(
        _inproj_kernel,
        out_shape=(f32_out, f32_out, head_out, head_out, head_t_out, f32_out, f32_out),
        grid=(b, l // tm),
        in_specs=[row,
                  pl.BlockSpec((None, 6, d), lambda bi, i: (bi, 0, 0)),
                  pl.BlockSpec((1, d), lambda bi, i: (0, 0)),
                  pl.BlockSpec((d, ncol), lambda bi, i: (0, 0), pipeline_mode=pl.Buffered(1)),
                  pl.BlockSpec((d, d), lambda bi, i: (0, 0), pipeline_mode=pl.Buffered(1))],
        out_specs=(row, row, head, head, head_t, row, row),
        compiler_params=_cparams("parallel", "parallel"),
        name="in_proj",
    )(x, mod, g, w_main_bf, w_vt_bf)


def _gelu_tanh(x):
    return x * (0.5 * (1.0 + jnp.tanh(math.sqrt(2.0 / math.pi) * (x + 0.044715 * (x * x * x)))))


def _scan_kernel(*refs, tl, nt, reverse):
    if reverse:
        (u_ref, prev_ref, next_ref, cw_ref, cb_ref, wbd_ref, br_ref, bi_ref, lam_ref,
         hf_ref, ug_ref, o_ref, carry_ref, h_ref) = refs
    else:
        (u_ref, prev_ref, next_ref, cw_ref, cb_ref, wbd_ref, br_ref, bi_ref, lam_ref,
         o_ref, carry_ref, h_ref) = refs
    c = D_RNN
    step = pl.program_id(1)
    t = (nt - 1 - step) if reverse else step

    @pl.when(step == 0)
    def _():
        carry_ref[...] = jnp.zeros_like(carry_ref)

    u = u_ref[...]
    row = lax.broadcasted_iota(jnp.int32, (tl, c), 0)
    halo_lo = jnp.where(t > 0, prev_ref[...], 0.0)
    halo_hi = jnp.where(t < nt - 1, next_ref[...], 0.0)
    xc = cb_ref[...] + cw_ref[2:3, :] * u
    for k in (2, 1):
        lo_fill = jnp.concatenate([pltpu.roll(halo_lo, k, 0)] + [halo_lo] * (tl // SUBLANES - 1), axis=0)
        shifted = jnp.where(row < k, lo_fill, pltpu.roll(u, k, 0))
        xc = xc + cw_ref[2 - k:3 - k, :] * shifted
    hi_fill = jnp.concatenate([halo_hi] * (tl // SUBLANES - 1) + [pltpu.roll(halo_hi, SUBLANES - 1, 0)],
                              axis=0)
    shifted = jnp.where(row >= tl - 1, hi_fill, pltpu.roll(u, tl - 1, 0))
    xc = xc + cw_ref[3:4, :] * shifted

    xcb = xc.astype(BF16)
    r_parts, i_parts = [], []
    for ch in range(c // GATE_CHUNK):
        z = _dot(xcb[:, ch * GATE_CHUNK:(ch + 1) * GATE_CHUNK], wbd_ref[ch])
        r_parts.append(z[:, :GATE_CHUNK])
        i_parts.append(z[:, GATE_CHUNK:])
    r = _sigmoid(jnp.concatenate(r_parts, axis=-1) + br_ref[...])
    ig = _sigmoid(jnp.concatenate(i_parts, axis=-1) + bi_ref[...])
    nl = -lam_ref[...]
    softplus = jnp.maximum(nl, 0.0) + jnp.log1p(jnp.exp(-jnp.abs(nl)))
    log_a = (-LRU_C) * r * softplus
    a = jnp.exp(log_a)
    one_m_a2 = -jnp.tanh(log_a) * (1.0 + a * a)
    root = jnp.where(one_m_a2 > 0.0, one_m_a2 * lax.rsqrt(one_m_a2), 0.0)
    bb = root * (ig * xc)

    ngroups = tl // SUBLANES
    a = a.reshape(ngroups, SUBLANES, c)
    bb = bb.reshape(ngroups, SUBLANES, c)
    pos = lax.broadcasted_iota(jnp.int32, (ngroups, SUBLANES, c), 1)
    s = 1
    while s < SUBLANES:
        if reverse:
            valid = pos < SUBLANES - s
            shift = SUBLANES - s
        else:
            valid = pos >= s
            shift = s
        a_n = jnp.where(valid, pltpu.roll(a, shift, 1), 1.0)
        b_n = jnp.where(valid, pltpu.roll(bb, shift, 1), 0.0)
        bb = a * b_n + bb
        a = a * a_n
        s *= 2
    carry = carry_ref[...]
    for step_g in range(ngroups):
        g = (ngroups - 1 - step_g) if reverse else step_g
        h_g = a[g] * carry + bb[g]
        h_ref[g * SUBLANES:(g + 1) * SUBLANES, :] = h_g
        carry = h_g[0:1, :] if reverse else h_g[SUBLANES - 1:SUBLANES, :]
    carry_ref[...] = carry

    if reverse:
        o_ref[...] = (_gelu_tanh(ug_ref[...]) * (hf_ref[...] + h_ref[...])).astype(o_ref.dtype)
    else:
        o_ref[...] = h_ref[...]


def _scan(u, conv_w, conv_b, wbd, b_r, b_i, lam, tl, h_fwd=None, u_gate=None):
    b, l, c = u.shape
    nt = l // tl
    reverse = h_fwd is not None
    tpb = tl // SUBLANES
    nblk = l // SUBLANES

    def tix(s):
        return (nt - 1 - s) if reverse else s

    row = pl.BlockSpec((None, tl, c), lambda bi, s: (bi, tix(s), 0))
    prev = pl.BlockSpec((None, SUBLANES, c), lambda bi, s: (bi, jnp.maximum(tix(s) * tpb - 1, 0), 0))
    nxt = pl.BlockSpec((None, SUBLANES, c),
                       lambda bi, s: (bi, jnp.minimum((tix(s) + 1) * tpb, nblk - 1), 0))
    vec = pl.BlockSpec((1, c), lambda bi, s: (0, 0))
    in_specs = [row, prev, nxt,
                pl.BlockSpec((4, c), lambda bi, s: (0, 0)), vec,
                pl.BlockSpec(wbd.shape, lambda bi, s: (0, 0, 0)), vec, vec, vec]
    args = [u, u, u, conv_w, conv_b, wbd, b_r, b_i, lam]
    if reverse:
        in_specs += [row, row]
        args += [h_fwd, u_gate]
        out_dtype = BF16
    else:
        out_dtype = F32
    return pl.pallas_call(
        functools.partial(_scan_kernel, tl=tl, nt=nt, reverse=reverse),
        out_shape=jax.ShapeDtypeStruct((b, l, c), out_dtype),
        grid=(b, nt),
        in_specs=in_specs,
        out_specs=row,
        scratch_shapes=[pltpu.VMEM((1, c), F32), pltpu.VMEM((tl, c), F32)],
        compiler_params=_cparams("parallel", "arbitrary"),
        name="rglru_bwd" if reverse else "rglru_fwd",
    )(*args)


def _bias_kernel(z_ref, o_ref, *, t, rc):
    y = jnp.broadcast_to(z_ref[...], (LANES, 2 * t))
    row = lax.broadcasted_iota(jnp.int32, (LANES, 2 * t), 0)
    bit = 0
    while (1 << bit) < LANES:
        y = jnp.where(((row >> bit) & 1) == 1, pltpu.roll(y, 1 << bit, 1), y)
        bit += 1
    for rh in range(t // LANES):
        blk = pltpu.roll(y, LANES * rh, 1) if rh else y
        for c in range(t // rc):
            o_ref[c, rh * LANES:(rh + 1) * LANES, :] = blk[:, c * rc:(c + 1) * rc]


def _bias_tiles(z, t, rc):
    nh = z.shape[0]
    return pl.pallas_call(
        functools.partial(_bias_kernel, t=t, rc=rc),
        out_shape=jax.ShapeDtypeStruct((nh, 3, t // rc, t, rc), F32),
        grid=(nh, 3),
        in_specs=[pl.BlockSpec((None, None, 1, 2 * t), lambda h, o: (h, o, 0, 0))],
        out_specs=pl.BlockSpec((None, None, t // rc, t, rc), lambda h, o: (h, o, 0, 0, 0)),
        compiler_params=_cparams("parallel", "parallel"),
        name="bias_tiles",
    )(z)


def _reduce_row_groups(s, op):
    parts = [s[r:r + SUBLANES, :] for r in range(0, s.shape[0], SUBLANES)]
    while len(parts) > 1:
        parts = [op(a, b) for a, b in zip(parts[0::2], parts[1::2])] + \
                ([parts[-1]] if len(parts) % 2 else [])
    return parts[0]


def _attn_kernel(far_ref, q_ref, k_ref, vt_ref, bias_ref, lamp_ref, g_ref, o_ref,
                 qz_ref, vta_ref, m_ref, alpha_ref, acc_ref, s_ref, *, nk, rc):
    hp = pl.program_id(1)
    i = pl.program_id(2)
    j = pl.program_id(3)
    nhs, t = q_ref.shape[0], q_ref.shape[1]
    nc = t // rc

    nunits = nhs * 2 * nc
    nkb = t // KEY_BLOCK
    assert nunits % S_BUFFERS == 0 and PIPE_SKEW < S_BUFFERS
    cur = j % 2
    prev = 1 - cur

    def unit(u):
        hh, rem = divmod(u, 2 * nc)
        c, br = divmod(rem, 2)
        return hh, c, br

    @pl.when(j == 0)
    def _():
        q = q_ref[...]
        lane = lax.broadcasted_iota(jnp.int32, q.shape, 2)
        qz_ref[0] = jnp.where(lane < HEAD_DIM, q, jnp.zeros_like(q))
        qz_ref[1] = jnp.where(lane >= HEAD_DIM, q, jnp.zeros_like(q))
        m_ref[...] = jnp.full_like(m_ref, -jnp.inf)
        alpha_ref[...] = jnp.ones_like(alpha_ref)
        acc_ref[...] = jnp.zeros_like(acc_ref)
        for u in range(nunits - PIPE_SKEW, nunits):
            s_ref[u % S_BUFFERS] = jnp.zeros(s_ref.shape[1:], F32)
        ones_row = lax.broadcasted_iota(jnp.int32, (V_ROWS - V_DIM, t), 0) == 0
        for side in range(2):
            for hh in range(nhs):
                vta_ref[side, hh, 0:V_DIM, :] = jnp.zeros((V_DIM, t), BF16)
                vta_ref[side, hh, V_DIM:, :] = jnp.where(ones_row, 1.0, 0.0).astype(BF16)

    for hh in range(nhs):
        vta_ref[cur, hh, 0:V_DIM, :] = vt_ref[hh]

    enter_band = j == jnp.maximum(i - 1, 0)
    leave_band = j == jnp.minimum(i + 1, nk - 1) + 1
    shift = [jnp.where(enter_band, far_ref[0, hp * nhs + hh], 0.0)
             - jnp.where(leave_band, far_ref[1, hp * nhs + hh], 0.0) for hh in range(nhs)]

    def value_stage_begin(u, carried):
        hh, c, br = unit(u)
        m_u = m_ref[hh, br, c]
        if carried:
            m_u = jnp.where(j > 0, m_u, jnp.inf)
        return m_u

    def value_stage_block(u, m_u, side, ks, accp):
        p = jnp.exp2(s_ref[u % S_BUFFERS, ks, :] - m_u)
        pv = _dot(vta_ref[side, unit(u)[0], :, ks], p.astype(BF16))
        return pv if accp is None else accp + pv

    def value_stage_end(u, accp):
        hh, c, br = unit(u)
        acc_ref[hh, br, c] = alpha_ref[hh, br, c] * acc_ref[hh, br, c] + accp

    def step(with_bias):
        for slot in range(nunits):
            ha, ca, bra = unit(slot)
            qa = qz_ref[bra, ha, ca * rc:(ca + 1) * rc, :]
            colmax = None
            carried = slot < PIPE_SKEW
            ub = slot - PIPE_SKEW + (nunits if carried else 0)
            side = prev if carried else cur
            m_b = value_stage_begin(ub, carried)
            accp = None
            for kb in range(nkb):
                ks = slice(kb * KEY_BLOCK, (kb + 1) * KEY_BLOCK)
                s = _dot_nt(k_ref[ha, ks, :], qa)
                if with_bias:
                    s = s + bias_ref[ha, ca, ks, :]
                s_ref[slot % S_BUFFERS, ks, :] = s
                bmax = _reduce_row_groups(s, jnp.maximum)
                colmax = bmax if colmax is None else jnp.maximum(colmax, bmax)
                accp = value_stage_block(ub, m_b, side, ks, accp)
            m_old = m_ref[ha, bra, ca] + shift[ha]
            m_new = jnp.maximum(m_old, jnp.max(colmax, axis=0, keepdims=True))
            alpha_ref[ha, bra, ca] = jnp.exp2(m_old - m_new)
            m_ref[ha, bra, ca] = m_new
            value_stage_end(ub, accp)

    is_band = jnp.abs(j - i) <= 1

    @pl.when(is_band)
    def _():
        step(True)

    @pl.when(jnp.logical_not(is_band))
    def _():
        step(False)

    @pl.when(j == nk - 1)
    def _():
        for u in range(nunits - PIPE_SKEW, nunits):
            m_u = value_stage_begin(u, False)
            accp = None
            for kb in range(nkb):
                accp = value_stage_block(u, m_u, cur, slice(kb * KEY_BLOCK, (kb + 1) * KEY_BLOCK), accp)
            value_stage_end(u, accp)
        lp = lamp_ref[...]
        lam = (jnp.exp(jnp.sum(lp[0:1, :] * lp[1:2, :], axis=-1, keepdims=True))
               - jnp.exp(jnp.sum(lp[2:3, :] * lp[3:4, :], axis=-1, keepdims=True)) + LAM_INIT)
        for hh in range(nhs):
            for c in range(nc):
                o = (acc_ref[hh, 0, c, 0:V_DIM, :] / acc_ref[hh, 0, c, V_DIM:V_DIM + 1, :]
                     - lam * (acc_ref[hh, 1, c, 0:V_DIM, :] / acc_ref[hh, 1, c, V_DIM:V_DIM + 1, :]))
                o = o * lax.rsqrt(jnp.mean(o * o, axis=0, keepdims=True) + EPS) * g_ref[...]
                o = o * (1.0 - LAM_INIT)
                o_ref[hh, c * rc:(c + 1) * rc, :] = o.T.astype(o_ref.dtype)


def _attention(q, k, vt, bias_tiles, far_bias, lam_params, subln_g_col, t, rc):
    b, nh, l, dv = q.shape
    nq = l // t
    nc = t // rc
    nhs = ATTN_HEADS_PER_STEP
    assert nh % nhs == 0
    qspec = pl.BlockSpec((None, nhs, t, dv), lambda bi, h, i, j: (bi, h, i, 0))
    kspec = pl.BlockSpec((None, nhs, t, dv), lambda bi, h, i, j: (bi, h, j, 0))
    vspec = pl.BlockSpec((None, nhs, dv, t), lambda bi, h, i, j: (bi, h, 0, j))
    return pl.pallas_call(
        functools.partial(_attn_kernel, nk=nq, rc=rc),
        out_shape=jax.ShapeDtypeStruct((b, nh, l, dv), BF16),
        grid=(b, nh // nhs, nq, nq),
        in_specs=[pl.BlockSpec(memory_space=pltpu.SMEM),
                  qspec, kspec, vspec,
                  pl.BlockSpec((nhs, None, nc, t, rc),
                               lambda bi, h, i, j: (h, jnp.clip(j - i, -1, 1) + 1, 0, 0, 0)),
                  pl.BlockSpec((4, HEAD_DIM), lambda bi, h, i, j: (0, 0)),
                  pl.BlockSpec((dv, 1), lambda bi, h, i, j: (0, 0))],
        out_specs=qspec,
        scratch_shapes=[pltpu.VMEM((2, nhs, t, dv), BF16), pltpu.VMEM((2, nhs, V_ROWS, t), BF16),
                        pltpu.VMEM((nhs, 2, nc, 1, rc), F32), pltpu.VMEM((nhs, 2, nc, 1, rc), F32),
                        pltpu.VMEM((nhs, 2, nc, V_ROWS, rc), F32),
                        pltpu.VMEM((S_BUFFERS, t, rc), F32)],
        compiler_params=_cparams("parallel", "parallel", "parallel", "arbitrary"),
        name="diff_attn",
    )(far_bias, q, k, vt, bias_tiles, lam_params, subln_g_col)


def _merge_kernel(ga_ref, o_ref, gma_ref, gmb_ref, x_ref, mod_ref, g_ref, wrnn_ref, wattn_ref,
                  wo_ref, wrh_ref, wrl_ref, x1_ref, h2_ref, s_ref):
    ya = _dot(ga_ref[...], wrnn_ref[...])
    ob = jnp.concatenate([o_ref[hd] for hd in range(N_HEADS)], axis=-1)
    yb = _dot(ob, wattn_ref[...])
    mixed = (gma_ref[...] * ya + gmb_ref[...] * yb).astype(BF16)
    x1 = x_ref[...] + mod_ref[2:3, :] * _dot(mixed, wo_ref[...])
    x1_ref[...] = x1
    h2 = _rms(x1, g_ref[...]) * (1.0 + mod_ref[4:5, :]) + mod_ref[3:4, :]
    h2_hi, h2_lo = _split_bf16(h2)
    h2_ref[...] = h2_hi
    logits = (_dot_nt(wrh_ref[...], h2_hi) + _dot_nt(wrh_ref[...], h2_lo)
              + _dot_nt(wrl_ref[...], h2_hi))
    s_ref[...] = _sigmoid(logits)


def _merge(ga, o, gm_a, gm_b, x, mod, g, w_rnn_bf, w_attn_bf, w_o_bf, wr_hi, wr_lo, tm):
    b, l, d = x.shape
    row = pl.BlockSpec((None, tm, d), lambda bi, i: (bi, i, 0))
    wspec = pl.BlockSpec((d, d), lambda bi, i: (0, 0), pipeline_mode=pl.Buffered(1))
    rspec = pl.BlockSpec((N_EXPERTS, d), lambda bi, i: (0, 0))
    return pl.pallas_call(
        _merge_kernel,
        out_shape=(jax.ShapeDtypeStruct((b, l, d), F32),
                   jax.ShapeDtypeStruct((b, l, d), BF16),
                   jax.ShapeDtypeStruct((b, N_EXPERTS, l), F32)),
        grid=(b, l // tm),
        in_specs=[row,
                  pl.BlockSpec((None, N_HEADS, tm, V_DIM), lambda bi, i: (bi, 0, i, 0)),
                  row, row, row,
                  pl.BlockSpec((None, 6, d), lambda bi, i: (bi, 0, 0)),
                  pl.BlockSpec((1, d), lambda bi, i: (0, 0)),
                  wspec, wspec, wspec, rspec, rspec],
        out_specs=(row, row, pl.BlockSpec((None, N_EXPERTS, tm), lambda bi, i: (bi, 0, i))),
        compiler_params=_cparams("parallel", "parallel"),
        name="merge_router",
    )(ga, o, gm_a, gm_b, x, mod, g, w_rnn_bf, w_attn_bf, w_o_bf, wr_hi, wr_lo)


def _route_kernel(s_ref, rb_ref, w_ref):
    tr = s_ref.shape[1]
    neg = -jnp.inf
    sub = lax.broadcasted_iota(jnp.int32, (GROUP_SIZE, tr), 0)
    s_g = [s_ref[g * GROUP_SIZE:(g + 1) * GROUP_SIZE, :] for g in range(N_GROUPS)]
    sb_g = [s_g[g] + rb_ref[g * GROUP_SIZE:(g + 1) * GROUP_SIZE, :] for g in range(N_GROUPS)]

    gscore = []
    for g in range(N_GROUPS):
        xg = sb_g[g]
        top1 = jnp.max(xg, axis=0, keepdims=True)
        first = jnp.min(jnp.where(xg == top1, sub, GROUP_SIZE), axis=0, keepdims=True)
        top2 = jnp.max(jnp.where(sub == first, neg, xg), axis=0, keepdims=True)
        gscore.append(top1 + top2)

    masked = []
    for g in range(N_GROUPS):
        rank = jnp.zeros((1, tr), jnp.int32)
        for g2 in range(N_GROUPS):
            if g2 == g:
                continue
            ahead = (gscore[g2] >= gscore[g]) if g2 < g else (gscore[g2] > gscore[g])
            rank = rank + ahead.astype(jnp.int32)
        keep = jnp.broadcast_to(rank < TOPK_GROUPS, (GROUP_SIZE, tr))
        masked.append(jnp.where(keep, sb_g[g], neg))

    ranks = [jnp.zeros((GROUP_SIZE, tr), jnp.int32) for _ in range(N_GROUPS)]
    for g2 in range(N_GROUPS):
        for r2 in range(GROUP_SIZE):
            other = jnp.broadcast_to(masked[g2][r2:r2 + 1, :], (GROUP_SIZE, tr))
            for g in range(N_GROUPS):
                mine = masked[g]
                if g2 < g:
                    ahead = other >= mine
                elif g2 > g:
                    ahead = other > mine
                else:
                    ahead = (other > mine) | ((other == mine) & (sub > r2))
                ranks[g] = ranks[g] + ahead.astype(jnp.int32)

    picked = [jnp.where(ranks[g] < TOP_K, s_g[g], 0.0) for g in range(N_GROUPS)]
    total = picked[0]
    for g in range(1, N_GROUPS):
        total = total + picked[g]
    denom = jnp.sum(total, axis=0, keepdims=True)
    for g in range(N_GROUPS):
        w_ref[g * GROUP_SIZE:(g + 1) * GROUP_SIZE, :] = picked[g] / denom * ROUTED_SCALE
    w_ref[N_EXPERTS:N_EXPERTS + GROUP_SIZE, :] = jnp.where(sub == 0, 1.0, 0.0)
    w_ref[N_EXPERTS + GROUP_SIZE:, :] = jnp.zeros((WEIGHT_COLS - N_EXPERTS - GROUP_SIZE, tr), F32)


def _route(s_t, router_bias_col, tr):
    b, ne, l = s_t.shape
    return pl.pallas_call(
        _route_kernel,
        out_shape=jax.ShapeDtypeStruct((b, WEIGHT_COLS, l), F32),
        grid=(b, l // tr),
        in_specs=[pl.BlockSpec((None, ne, tr), lambda bi, i: (bi, 0, i)),
                  pl.BlockSpec((ne, 1), lambda bi, i: (0, 0))],
        out_specs=pl.BlockSpec((None, WEIGHT_COLS, tr), lambda bi, i: (bi, 0, i)),
        compiler_params=_cparams("parallel", "parallel"),
        name="route_topk",
    )(s_t, router_bias_col)


def _moe_kernel(h_ref, w_ref, x1_ref, mod_ref, g_ref, w1_ref, w3_ref, w2_ref, w2p_ref,
                w1s_ref, w3s_ref, w2s_ref, o_ref, acc_ref, hid_ref, *, nsteps, rc):
    st = pl.program_id(2)
    tm = h_ref.shape[0]
    eps = MOE_EXPERTS_PER_STEP
    chunks = [slice(r * rc, (r + 1) * rc) for r in range(tm // rc)]

    @pl.when(st == 0)
    def _():
        acc_ref[...] = jnp.zeros_like(acc_ref)
        hid_ref[...] = jnp.zeros_like(hid_ref)

    w = w_ref[...]
    lane = lax.broadcasted_iota(jnp.int32, w.shape, 1)

    def up_chunk(slot, w1, w3, wcol, rows):
        h = h_ref[rows, :]
        a1 = _dot(h, w1)
        hidden = (a1 * _sigmoid(a1)) * _dot(h, w3)
        if wcol is not None:
            hidden = hidden * wcol[rows, :]
        hid_ref[slot % 2, rows, :] = hidden.astype(BF16)

    def down_chunk(slot, w2, rows):
        acc_ref[rows, :] += _dot(hid_ref[slot % 2, rows, :], w2)

    for s in range(eps):
        wcol = jnp.sum(jnp.where(lane == st * eps + s, w, 0.0), axis=-1, keepdims=True)
        w2_prev = w2p_ref[...] if s == 0 else w2_ref[s - 1]
        for rows in chunks:
            up_chunk(s, w1_ref[s], w3_ref[s], wcol, rows)
            down_chunk(s - 1, w2_prev, rows)

    @pl.when(st == nsteps - 1)
    def _():
        for rows in chunks:
            up_chunk(eps, w1s_ref[...], w3s_ref[...], None, rows)
            down_chunk(eps - 1, w2_ref[eps - 1], rows)
        for rows in chunks:
            down_chunk(eps, w2s_ref[...], rows)
        y = x1_ref[...] + mod_ref[5:6, :] * acc_ref[...]
        o_ref[...] = _rms(y, g_ref[...])


def _moe(h2, w_tok, x1, mod, final_g, w1_all, w3_all, w2_all, w1_s, w3_s, w2_s, tm, rc):
    b, l, d = x1.shape
    eps = MOE_EXPERTS_PER_STEP
    assert eps % 2 == 0 and w1_all.shape[0] % eps == 0
    nsteps = w1_all.shape[0] // eps
    row = pl.BlockSpec((None, tm, d), lambda bi, i, e: (bi, i, 0))
    up_s = pl.BlockSpec((d, D_EXPERT), lambda bi, i, e: (0, 0))
    return pl.pallas_call(
        functools.partial(_moe_kernel, nsteps=nsteps, rc=rc),
        out_shape=jax.ShapeDtypeStruct((b, l, d), F32),
        grid=(b, l // tm, nsteps),
        in_specs=[pl.BlockSpec((None, tm, d), lambda bi, i, e: (bi, i, 0)),
                  pl.BlockSpec((None, tm, WEIGHT_COLS), lambda bi, i, e: (bi, i, 0)),
                  row,
                  pl.BlockSpec((None, 6, d), lambda bi, i, e: (bi, 0, 0)),
                  pl.BlockSpec((1, d), lambda bi, i, e: (0, 0)),
                  pl.BlockSpec((eps, d, D_EXPERT), lambda bi, i, e: (e, 0, 0)),
                  pl.BlockSpec((eps, d, D_EXPERT), lambda bi, i, e: (e, 0, 0)),
                  pl.BlockSpec((eps, D_EXPERT, d), lambda bi, i, e: (e, 0, 0)),
                  pl.BlockSpec((None, D_EXPERT, d), lambda bi, i, e: (jnp.maximum(e * eps - 1, 0), 0, 0)),
                  up_s, up_s, pl.BlockSpec((D_EXPERT, d), lambda bi, i, e: (0, 0))],
        out_specs=row,
        scratch_shapes=[pltpu.VMEM((tm, d), F32), pltpu.VMEM((2, tm, D_EXPERT), BF16)],
        compiler_params=_cparams("parallel", "parallel", "arbitrary"),
        name="moe_experts",
    )(h2, w_tok, x1, mod, final_g, w1_all, w3_all, w2_all, w2_all, w1_s, w3_s, w2_s)


def _t5_bucket_np(rel):
    half = N_BUCKETS // 2
    max_exact = half // 2
    ret = np.where(rel > 0, half, 0)
    n = np.abs(rel)
    nf = np.maximum(n, 1).astype(np.float32)
    large = max_exact + (np.log(nf / np.float32(max_exact)) / np.float32(math.log(MAX_DISTANCE / max_exact))
                         * np.float32(half - max_exact)).astype(np.int32)
    large = np.minimum(large, half - 1)
    return (ret + np.where(n < max_exact, n, large)).astype(np.int32)


def _bias_tables(rel_bias, t):
    assert t >= MAX_DISTANCE
    v = np.arange(2 * t)
    d = np.where(v < t, v, v - 2 * t)
    rel = np.stack([off - d for off in (-t, 0, t)])
    scaled = rel_bias.astype(F32) * LOG2E
    z = jnp.take(scaled, jnp.asarray(_t5_bucket_np(rel)), axis=0)
    z = jnp.transpose(z, (2, 0, 1))[:, :, None, :]
    far = jnp.stack([scaled[int(_t5_bucket_np(np.array(-MAX_DISTANCE)))],
                     scaled[int(_t5_bucket_np(np.array(MAX_DISTANCE)))]])
    return z, far


def _block_diag_gates(w_r, w_i):
    per = GATE_CHUNK // LRU_BLOCK
    eye = jnp.eye(per, dtype=F32)

    def bd(w):
        w = w.reshape(-1, per, LRU_BLOCK, LRU_BLOCK)
        return jnp.einsum('cjab,jk->cjakb', w, eye).reshape(-1, GATE_CHUNK, GATE_CHUNK)

    return jnp.concatenate([bd(w_r), bd(w_i)], axis=-1).astype(BF16)


def _tile(l, pref):
    return min(l, pref)


def _trunk(x, mod, p, tiles=None):
    b, l, d = x.shape
    tl = dict(inproj=512, scan=256, attn=1024, attn_rows=256, merge=512, route=512, moe=1024)
    if tiles:
        tl.update(tiles)
    tl = {k: _tile(l, v) for k, v in tl.items()}
    tl['attn_rows'] = min(tl['attn_rows'], tl['attn'])

    u_rnn, u_gate, q, k, vt, gm_a, gm_b = _inproj(x, mod, p['norm_mix_g'], p['w_main'], p['w_vt'],
                                                  tl['inproj'])

    h_fwd = _scan(u_rnn, p['conv_w'], p['conv_b'], p['wbd'][0], p['b_r'][0:1], p['b_i'][0:1],
                  p['lam'][0:1], tl['scan'])
    ga = _scan(u_rnn, p['conv_w'], p['conv_b'], p['wbd'][1], p['b_r'][1:2], p['b_i'][1:2],
               p['lam'][1:2], tl['scan'], h_fwd=h_fwd, u_gate=u_gate)

    bias_tiles, far = p['bias'](tl['attn'], tl['attn_rows'])
    o = _attention(q, k, vt, bias_tiles, far, p['lam_params'], p['subln_g'], tl['attn'], tl['attn_rows'])

    x1, h2, s_t = _merge(ga, o, gm_a, gm_b, x, mod, p['norm_ffn_g'], p['w_rnn_out'], p['w_attn_out'],
                         p['w_o'], p['wr_hi'], p['wr_lo'], tl['merge'])
    w_t = _route(s_t, p['router_bias'], tl['route'])
    w_tok = jnp.swapaxes(w_t, 1, 2)
    return _moe(h2, w_tok, x1, mod, p['final_norm_g'], p['w1'], p['w3'], p['w2'],
                p['w1_s'], p['w3_s'], p['w2_s'], tl['moe'], min(tl['moe'], MOE_ROW_CHUNK))


def _prepare(norm_mix_g, norm_ffn_g, final_norm_g, w_in, conv_w, conv_b, w_rgate, b_rgate, w_igate,
             b_igate, lru_lambda, w_rnn_out, lambda_q1, lambda_k1, lambda_q2, lambda_k2, subln_g,
             rel_bias, w_attn_out, w_o, w_router, router_bias, w1_e, w3_e, w2_e, w1_s, w3_s, w2_s):
    wr_t = w_router[0].T.astype(F32)
    wr_hi = wr_t.astype(BF16)
    wr_lo = (wr_t - wr_hi.astype(F32)).astype(BF16)
    bias_cache = {}

    def bias(t, rc):
        if (t, rc) not in bias_cache:
            z, far = _bias_tables(rel_bias, t)
            bias_cache[(t, rc)] = (_bias_tiles(z, t, rc), far)
        return bias_cache[(t, rc)]

    w_in0 = w_in[0]
    v_lo = 2 * D_RNN + 2 * N_HEADS * 2 * HEAD_DIM
    v_hi = v_lo + N_HEADS * V_DIM
    return dict(
        norm_mix_g=norm_mix_g[0][None], norm_ffn_g=norm_ffn_g[0][None], final_norm_g=final_norm_g[None],
        w_main=jnp.concatenate([w_in0[:, :v_lo], w_in0[:, v_hi:]], axis=1).astype(BF16),
        w_vt=w_in0[:, v_lo:v_hi].T.astype(BF16),
        conv_w=conv_w[0], conv_b=conv_b[0][None],
        wbd=jnp.stack([_block_diag_gates(w_rgate[0, dr], w_igate[0, dr]) for dr in range(2)]),
        b_r=b_rgate[0], b_i=b_igate[0], lam=lru_lambda[0],
        w_rnn_out=w_rnn_out[0].astype(BF16), w_attn_out=w_attn_out[0].astype(BF16),
        w_o=w_o[0].astype(BF16),
        lam_params=jnp.stack([lambda_q1[0], lambda_k1[0], lambda_q2[0], lambda_k2[0]]).astype(F32),
        subln_g=subln_g[0][:, None].astype(F32), bias=bias,
        wr_hi=wr_hi, wr_lo=wr_lo, router_bias=router_bias[0][:, None].astype(F32),
        w1=w1_e[0].astype(BF16), w3=w3_e[0].astype(BF16), w2=w2_e[0].astype(BF16),
        w1_s=w1_s[0].astype(BF16), w3_s=w3_s[0].astype(BF16), w2_s=w2_s[0].astype(BF16),
    )


def kernel(x_prompt, x_sample, c_prompt, c_sample, norm_mix_g, norm_ffn_g, final_norm_g, w_ada, b_ada, w_in, conv_w, conv_b, w_rgate, b_rgate, w_igate, b_igate, lru_lambda, w_rnn_out, lambda_q1, lambda_k1, lambda_q2, lambda_k2, subln_g, rel_bias, w_attn_out, w_o, w_router, router_bias, w1_e, w3_e, w2_e, w1_s, w3_s, w2_s):
    d = D_MODEL
    nb_p, nb_s = c_prompt.shape[0], c_sample.shape[0]
    pad = (-(nb_p + nb_s)) % SUBLANES
    c_all = jnp.concatenate([c_prompt, c_sample, jnp.zeros((pad, d), F32)], axis=0)
    mod = _ada(c_all, w_ada[0], b_ada[0][None]).reshape(-1, 6, d)
    p = _prepare(norm_mix_g, norm_ffn_g, final_norm_g, w_in, conv_w, conv_b, w_rgate, b_rgate,
                 w_igate, b_igate, lru_lambda, w_rnn_out, lambda_q1, lambda_k1, lambda_q2, lambda_k2,
                 subln_g, rel_bias, w_attn_out, w_o, w_router, router_bias, w1_e, w3_e, w2_e,
                 w1_s, w3_s, w2_s)
    y_prompt = _trunk(x_prompt, mod[:nb_p], p)
    y_sample = _trunk(x_sample, mod[nb_p:nb_p + nb_s], p)
    return (y_prompt, y_sample)
```

```python
import functools
import math

import numpy as np
import jax
import jax.numpy as jnp
from jax import lax
from jax.experimental import pallas as pl
from jax.experimental.pallas import tpu as pltpu

D_MODEL = 1024
D_RNN = 1024
LRU_BLOCK = 64
LRU_C = 8.0
N_HEADS = 8
HEAD_DIM = 64
V_DIM = 2 * HEAD_DIM
N_BUCKETS = 32
MAX_DISTANCE = 128
N_EXPERTS = 64
TOP_K = 8
N_GROUPS = 8
GROUP_SIZE = N_EXPERTS // N_GROUPS
TOPK_GROUPS = 4
D_EXPERT = 256
ROUTED_SCALE = 2.5
EPS = 1e-6
LAM_INIT = 0.8 - 0.6 * math.exp(-0.3 * 0)

VMEM_LIMIT_BYTES = 56 * 1024 * 1024
LANES = 128
SUBLANES = 8
GATE_CHUNK = 256
MOE_EXPERTS_PER_STEP = 8
MOE_ROW_CHUNK = 256
KEY_BLOCK = 256
ATTN_HEADS_PER_STEP = 4
PIPE_SKEW = 2
S_BUFFERS = 4
V_ROWS = V_DIM + 16
LOG2E = math.log2(math.e)
WEIGHT_COLS = 128

F32 = jnp.float32
BF16 = jnp.bfloat16


def _cparams(*sem):
    return pltpu.CompilerParams(dimension_semantics=sem, vmem_limit_bytes=VMEM_LIMIT_BYTES)


def _dot(a, b):
    return jnp.dot(a, b, preferred_element_type=F32)


def _dot_nt(a, b):
    return lax.dot_general(a, b, (((1,), (1,)), ((), ())), preferred_element_type=F32)


def _split_bf16(x):
    hi = x.astype(BF16)
    lo = (x - hi.astype(F32)).astype(BF16)
    return hi, lo


def _sigmoid(x):
    return 1.0 / (1.0 + jnp.exp(-x))


def _rms(x, g):
    return x * lax.rsqrt(jnp.mean(x * x, axis=-1, keepdims=True) + EPS) * g


def _ada_kernel(c_ref, w_ref, b_ref, o_ref):
    c = c_ref[...]
    sc = c * _sigmoid(c)
    c_hi, c_lo = _split_bf16(sc)
    w_hi, w_lo = _split_bf16(w_ref[...])
    o_ref[...] = _dot(c_hi, w_hi) + _dot(c_lo, w_hi) + _dot(c_hi, w_lo) + b_ref[...]


def _ada(c_all, w_ada, b_ada, tn=1536):
    rows, d = c_all.shape
    n = w_ada.shape[1]
    return pl.pallas_call(
        _ada_kernel,
        out_shape=jax.ShapeDtypeStruct((rows, n), F32),
        grid=(n // tn,),
        in_specs=[pl.BlockSpec((rows, d), lambda j: (0, 0)),
                  pl.BlockSpec((d, tn), lambda j: (0, j)),
                  pl.BlockSpec((1, tn), lambda j: (0, j))],
        out_specs=pl.BlockSpec((rows, tn), lambda j: (0, j)),
        compiler_params=_cparams("parallel"),
        name="ada_mod",
    )(c_all, w_ada, b_ada)


def _inproj_kernel(x_ref, mod_ref, g_ref, w_ref, wvt_ref, urnn_ref, ugate_ref, q_ref, k_ref, vt_ref,
                   ga_ref, gb_ref):
    d = D_MODEL
    x = x_ref[...]
    h = _rms(x, g_ref[...]) * (1.0 + mod_ref[1:2, :]) + mod_ref[0:1, :]
    hb = h.astype(BF16)

    def col(c):
        return _dot(hb, w_ref[:, c * d:(c + 1) * d])

    urnn_ref[...] = col(0)
    ugate_ref[...] = col(1)
    q = (col(2) * (HEAD_DIM ** -0.5 * LOG2E)).astype(BF16)
    k = col(3).astype(BF16)
    vt = _dot_nt(wvt_ref[...], hb).astype(BF16)
    for hd in range(N_HEADS):
        sl = slice(hd * V_DIM, (hd + 1) * V_DIM)
        q_ref[hd] = q[:, sl]
        k_ref[hd] = k[:, sl]
        vt_ref[hd] = vt[sl, :]
    ga_ref[...] = _sigmoid(col(5))
    gb_ref[...] = _sigmoid(col(6))


def _inproj(x, mod, g, w_main_bf, w_vt_bf, tm):
    b, l, d = x.shape
    ncol = w_main_bf.shape[1]
    row = pl.BlockSpec((None, tm, d), lambda bi, i: (bi, i, 0))
    head = pl.BlockSpec((None, N_HEADS, tm, V_DIM), lambda bi, i: (bi, 0, i, 0))
    head_t = pl.BlockSpec((None, N_HEADS, V_DIM, tm), lambda bi, i: (bi, 0, 0, i))
    f32_out = jax.ShapeDtypeStruct((b, l, d), F32)
    head_out = jax.ShapeDtypeStruct((b, N_HEADS, l, V_DIM), BF16)
    head_t_out = jax.ShapeDtypeStruct((b, N_HEADS, V_DIM, l), BF16)
    return pl.pallas_call(
        _inproj_kernel,
        out_shape=(f32_out, f32_out, head_out, head_out, head_t_out, f32_out, f32_out),
        grid=(b, l // tm),
        in_specs=[row,
                  pl.BlockSpec((None, 6, d), lambda bi, i: (bi, 0, 0)),
                  pl.BlockSpec((1, d), lambda bi, i: (0, 0)),
                  pl.BlockSpec((d, ncol), lambda bi, i: (0, 0), pipeline_mode=pl.Buffered(1)),
                  pl.BlockSpec((d, d), lambda bi, i: (0, 0), pipeline_mode=pl.Buffered(1))],
        out_specs=(row, row, head, head, head_t, row, row),
        compiler_params=_cparams("parallel", "parallel"),
        name="in_proj",
    )(x, mod, g, w_main_bf, w_vt_bf)


def _gelu_tanh(x):
    return x * (0.5 * (1.0 + jnp.tanh(math.sqrt(2.0 / math.pi) * (x + 0.044715 * (x * x * x)))))


def _scan_kernel(*refs, tl, nt, reverse):
    if reverse:
        (u_ref, prev_ref, next_ref, cw_ref, cb_ref, wbd_ref, br_ref, bi_ref, lam_ref,
         hf_ref, ug_ref, o_ref, carry_ref, h_ref) = refs
    else:
        (u_ref, prev_ref, next_ref, cw_ref, cb_ref, wbd_ref, br_ref, bi_ref, lam_ref,
         o_ref, carry_ref, h_ref) = refs
    c = D_RNN
    step = pl.program_id(1)
    t = (nt - 1 - step) if reverse else step

    @pl.when(step == 0)
    def _():
        carry_ref[...] = jnp.zeros_like(carry_ref)

    u = u_ref[...]
    row = lax.broadcasted_iota(jnp.int32, (tl, c), 0)
    halo_lo = jnp.where(t > 0, prev_ref[...], 0.0)
    halo_hi = jnp.where(t < nt - 1, next_ref[...], 0.0)
    xc = cb_ref[...] + cw_ref[2:3, :] * u
    for k in (2, 1):
        lo_fill = jnp.concatenate([pltpu.roll(halo_lo, k, 0)] + [halo_lo] * (tl // SUBLANES - 1), axis=0)
        shifted = jnp.where(row < k, lo_fill, pltpu.roll(u, k, 0))
        xc = xc + cw_ref[2 - k:3 - k, :] * shifted
    hi_fill = jnp.concatenate([halo_hi] * (tl // SUBLANES - 1) + [pltpu.roll(halo_hi, SUBLANES - 1, 0)],
                              axis=0)
    shifted = jnp.where(row >= tl - 1, hi_fill, pltpu.roll(u, tl - 1, 0))
    xc = xc + cw_ref[3:4, :] * shifted

    xcb = xc.astype(BF16)
    r_parts, i_parts = [], []
    for ch in range(c // GATE_CHUNK):
        z = _dot(xcb[:, ch * GATE_CHUNK:(ch + 1) * GATE_CHUNK], wbd_ref[ch])
        r_parts.append(z[:, :GATE_CHUNK])
        i_parts.append(z[:, GATE_CHUNK:])
    r = _sigmoid(jnp.concatenate(r_parts, axis=-1) + br_ref[...])
    ig = _sigmoid(jnp.concatenate(i_parts, axis=-1) + bi_ref[...])
    nl = -lam_ref[...]
    softplus = jnp.maximum(nl, 0.0) + jnp.log1p(jnp.exp(-jnp.abs(nl)))
    log_a = (-LRU_C) * r * softplus
    a = jnp.exp(log_a)
    one_m_a2 = -jnp.tanh(log_a) * (1.0 + a * a)
    root = jnp.where(one_m_a2 > 0.0, one_m_a2 * lax.rsqrt(one_m_a2), 0.0)
    bb = root * (ig * xc)

    ngroups = tl // SUBLANES
    a = a.reshape(ngroups, SUBLANES, c)
    bb = bb.reshape(ngroups, SUBLANES, c)
    pos = lax.broadcasted_iota(jnp.int32, (ngroups, SUBLANES, c), 1)
    s = 1
    while s < SUBLANES:
        if reverse:
            valid = pos < SUBLANES - s
            shift = SUBLANES - s
        else:
            valid = pos >= s
            shift = s
        a_n = jnp.where(valid, pltpu.roll(a, shift, 1), 1.0)
        b_n = jnp.where(valid, pltpu.roll(bb, shift, 1), 0.0)
        bb = a * b_n + bb
        a = a * a_n
        s *= 2
    carry = carry_ref[...]
    for step_g in range(ngroups):
        g = (ngroups - 1 - step_g) if reverse else step_g
        h_g = a[g] * carry + bb[g]
        h_ref[g * SUBLANES:(g + 1) * SUBLANES, :] = h_g
        carry = h_g[0:1, :] if reverse else h_g[SUBLANES - 1:SUBLANES, :]
    carry_ref[...] = carry

    if reverse:
        o_ref[...] = (_gelu_tanh(ug_ref[...]) * (hf_ref[...] + h_ref[...])).astype(o_ref.dtype)
    else:
        o_ref[...] = h_ref[...]


def _scan(u, conv_w, conv_b, wbd, b_r, b_i, lam, tl, h_fwd=None, u_gate=None):
    b, l, c = u.shape
    nt = l // tl
    reverse = h_fwd is not None
    tpb = tl // SUBLANES
    nblk = l // SUBLANES

    def tix(s):
        return (nt - 1 - s) if reverse else s

    row = pl.BlockSpec((None, tl, c), lambda bi, s: (bi, tix(s), 0))
    prev = pl.BlockSpec((None, SUBLANES, c), lambda bi, s: (bi, jnp.maximum(tix(s) * tpb - 1, 0), 0))
    nxt = pl.BlockSpec((None, SUBLANES, c),
                       lambda bi, s: (bi, jnp.minimum((tix(s) + 1) * tpb, nblk - 1), 0))
    vec = pl.BlockSpec((1, c), lambda bi, s: (0, 0))
    in_specs = [row, prev, nxt,
                pl.BlockSpec((4, c), lambda bi, s: (0, 0)), vec,
                pl.BlockSpec(wbd.shape, lambda bi, s: (0, 0, 0)), vec, vec, vec]
    args = [u, u, u, conv_w, conv_b, wbd, b_r, b_i, lam]
    if reverse:
        in_specs += [row, row]
        args += [h_fwd, u_gate]
        out_dtype = BF16
    else:
        out_dtype = F32
    return pl.pallas_call(
        functools.partial(_scan_kernel, tl=tl, nt=nt, reverse=reverse),
        out_shape=jax.ShapeDtypeStruct((b, l, c), out_dtype),
        grid=(b, nt),
        in_specs=in_specs,
        out_specs=row,
        scratch_shapes=[pltpu.VMEM((1, c), F32), pltpu.VMEM((tl, c), F32)],
        compiler_params=_cparams("parallel", "arbitrary"),
        name="rglru_bwd" if reverse else "rglru_fwd",
    )(*args)


def _bias_kernel(z_ref, o_ref, *, t, rc):
    y = jnp.broadcast_to(z_ref[...], (LANES, 2 * t))
    row = lax.broadcasted_iota(jnp.int32, (LANES, 2 * t), 0)
    bit = 0
    while (1 << bit) < LANES:
        y = jnp.where(((row >> bit) & 1) == 1, pltpu.roll(y, 1 << bit, 1), y)
        bit += 1
    for rh in range(t // LANES):
        blk = pltpu.roll(y, LANES * rh, 1) if rh else y
        for c in range(t // rc):
            o_ref[c, rh * LANES:(rh + 1) * LANES, :] = blk[:, c * rc:(c + 1) * rc]


def _bias_tiles(z, t, rc):
    nh = z.shape[0]
    return pl.pallas_call(
        functools.partial(_bias_kernel, t=t, rc=rc),
        out_shape=jax.ShapeDtypeStruct((nh, 3, t // rc, t, rc), F32),
        grid=(nh, 3),
        in_specs=[pl.BlockSpec((None, None, 1, 2 * t), lambda h, o: (h, o, 0, 0))],
        out_specs=pl.BlockSpec((None, None, t // rc, t, rc), lambda h, o: (h, o, 0, 0, 0)),
        compiler_params=_cparams("parallel", "parallel"),
        name="bias_tiles",
    )(z)


def _reduce_row_groups(s, op):
    parts = [s[r:r + SUBLANES, :] for r in range(0, s.shape[0], SUBLANES)]
    while len(parts) > 1:
        parts = [op(a, b) for a, b in zip(parts[0::2], parts[1::2])] + \
                ([parts[-1]] if len(parts) % 2 else [])
    return parts[0]


def _attn_kernel(far_ref, q_ref, k_ref, vt_ref, bias_ref, lamp_ref, g_ref, o_ref,
                 qz_ref, vta_ref, m_ref, alpha_ref, acc_ref, s_ref, *, nk, rc):
    hp = pl.program_id(1)
    i = pl.program_id(2)
    j = pl.program_id(3)
    nhs, t = q_ref.shape[0], q_ref.shape[1]
    nc = t // rc

    nunits = nhs * 2 * nc
    nkb = t // KEY_BLOCK
    assert nunits % S_BUFFERS == 0 and PIPE_SKEW < S_BUFFERS
    cur = j % 2
    prev = 1 - cur

    def unit(u):
        hh, rem = divmod(u, 2 * nc)
        c, br = divmod(rem, 2)
        return hh, c, br

    @pl.when(j == 0)
    def _():
        q = q_ref[...]
        lane = lax.broadcasted_iota(jnp.int32, q.shape, 2)
        qz_ref[0] = jnp.where(lane < HEAD_DIM, q, jnp.zeros_like(q))
        qz_ref[1] = jnp.where(lane >= HEAD_DIM, q, jnp.zeros_like(q))
        m_ref[...] = jnp.full_like(m_ref, -jnp.inf)
        alpha_ref[...] = jnp.ones_like(alpha_ref)
        acc_ref[...] = jnp.zeros_like(acc_ref)
        for u in range(nunits - PIPE_SKEW, nunits):
            s_ref[u % S_BUFFERS] = jnp.zeros(s_ref.shape[1:], F32)
        ones_row = lax.broadcasted_iota(jnp.int32, (V_ROWS - V_DIM, t), 0) == 0
        for side in range(2):
            for hh in range(nhs):
                vta_ref[side, hh, 0:V_DIM, :] = jnp.zeros((V_DIM, t), BF16)
                vta_ref[side, hh, V_DIM:, :] = jnp.where(ones_row, 1.0, 0.0).astype(BF16)

    for hh in range(nhs):
        vta_ref[cur, hh, 0:V_DIM, :] = vt_ref[hh]

    enter_band = j == jnp.maximum(i - 1, 0)
    leave_band = j == jnp.minimum(i + 1, nk - 1) + 1
    shift = [jnp.where(enter_band, far_ref[0, hp * nhs + hh], 0.0)
             - jnp.where(leave_band, far_ref[1, hp * nhs + hh], 0.0) for hh in range(nhs)]

    def value_stage_begin(u, carried):
        hh, c, br = unit(u)
        m_u = m_ref[hh, br, c]
        if carried:
            m_u = jnp.where(j > 0, m_u, jnp.inf)
        return m_u

    def value_stage_block(u, m_u, side, ks, accp):
        p = jnp.exp2(s_ref[u % S_BUFFERS, ks, :] - m_u)
        pv = _dot(vta_ref[side, unit(u)[0], :, ks], p.astype(BF16))
        return pv if accp is None else accp + pv

    def value_stage_end(u, accp):
        hh, c, br = unit(u)
        acc_ref[hh, br, c] = alpha_ref[hh, br, c] * acc_ref[hh, br, c] + accp

    def step(with_bias):
        for slot in range(nunits):
            ha, ca, bra = unit(slot)
            qa = qz_ref[bra, ha, ca * rc:(ca + 1) * rc, :]
            colmax = None
            carried = slot < PIPE_SKEW
            ub = slot - PIPE_SKEW + (nunits if carried else 0)
            side = prev if carried else cur
            m_b = value_stage_begin(ub, carried)
            accp = None
            for kb in range(nkb):
                ks = slice(kb * KEY_BLOCK, (kb + 1) * KEY_BLOCK)
                s = _dot_nt(k_ref[ha, ks, :], qa)
                if with_bias:
                    s = s + bias_ref[ha, ca, ks, :]
                s_ref[slot % S_BUFFERS, ks, :] = s
                bmax = _reduce_row_groups(s, jnp.maximum)
                colmax = bmax if colmax is None else jnp.maximum(colmax, bmax)
                accp = value_stage_block(ub, m_b, side, ks, accp)
            m_old = m_ref[ha, bra, ca] + shift[ha]
            m_new = jnp.maximum(m_old, jnp.max(colmax, axis=0, keepdims=True))
            alpha_ref[ha, bra, ca] = jnp.exp2(m_old - m_new)
            m_ref[ha, bra, ca] = m_new
            value_stage_end(ub, accp)

    is_band = jnp.abs(j - i) <= 1

    @pl.when(is_band)
    def _():
        step(True)

    @pl.when(jnp.logical_not(is_band))
    def _():
        step(False)

    @pl.when(j == nk - 1)
    def _():
        for u in range(nunits - PIPE_SKEW, nunits):
            m_u = value_stage_begin(u, False)
            accp = None
            for kb in range(nkb):
                accp = value_stage_block(u, m_u, cur, slice(kb * KEY_BLOCK, (kb + 1) * KEY_BLOCK), accp)
            value_stage_end(u, accp)
        lp = lamp_ref[...]
        lam = (jnp.exp(jnp.sum(lp[0:1, :] * lp[1:2, :], axis=-1, keepdims=True))
               - jnp.exp(jnp.sum(lp[2:3, :] * lp[3:4, :], axis=-1, keepdims=True)) + LAM_INIT)
        for hh in range(nhs):
            for c in range(nc):
                o = (acc_ref[hh, 0, c, 0:V_DIM, :] / acc_ref[hh, 0, c, V_DIM:V_DIM + 1, :]
                     - lam * (acc_ref[hh, 1, c, 0:V_DIM, :] / acc_ref[hh, 1, c, V_DIM:V_DIM + 1, :]))
                o = o * lax.rsqrt(jnp.mean(o * o, axis=0, keepdims=True) + EPS) * g_ref[...]
                o = o * (1.0 - LAM_INIT)
                o_ref[hh, c * rc:(c + 1) * rc, :] = o.T.astype(o_ref.dtype)


def _attention(q, k, vt, bias_tiles, far_bias, lam_params, subln_g_col, t, rc):
    b, nh, l, dv = q.shape
    nq = l // t
    nc = t // rc
    nhs = ATTN_HEADS_PER_STEP
    assert nh % nhs == 0
    qspec = pl.BlockSpec((None, nhs, t, dv), lambda bi, h, i, j: (bi, h, i, 0))
    kspec = pl.BlockSpec((None, nhs, t, dv), lambda bi, h, i, j: (bi, h, j, 0))
    vspec = pl.BlockSpec((None, nhs, dv, t), lambda bi, h, i, j: (bi, h, 0, j))
    return pl.pallas_call(
        functools.partial(_attn_kernel, nk=nq, rc=rc),
        out_shape=jax.ShapeDtypeStruct((b, nh, l, dv), BF16),
        grid=(b, nh // nhs, nq, nq),
        in_specs=[pl.BlockSpec(memory_space=pltpu.SMEM),
                  qspec, kspec, vspec,
                  pl.BlockSpec((nhs, None, nc, t, rc),
                               lambda bi, h, i, j: (h, jnp.clip(j - i, -1, 1) + 1, 0, 0, 0)),
                  pl.BlockSpec((4, HEAD_DIM), lambda bi, h, i, j: (0, 0)),
                  pl.BlockSpec((dv, 1), lambda bi, h, i, j: (0, 0))],
        out_specs=qspec,
        scratch_shapes=[pltpu.VMEM((2, nhs, t, dv), BF16), pltpu.VMEM((2, nhs, V_ROWS, t), BF16),
                        pltpu.VMEM((nhs, 2, nc, 1, rc), F32), pltpu.VMEM((nhs, 2, nc, 1, rc), F32),
                        pltpu.VMEM((nhs, 2, nc, V_ROWS, rc), F32),
                        pltpu.VMEM((S_BUFFERS, t, rc), F32)],
        compiler_params=_cparams("parallel", "parallel", "parallel", "arbitrary"),
        name="diff_attn",
    )(far_bias, q, k, vt, bias_tiles, lam_params, subln_g_col)


def _merge_kernel(ga_ref, o_ref, gma_ref, gmb_ref, x_ref, mod_ref, g_ref, wrnn_ref, wattn_ref,
                  wo_ref, wrh_ref, wrl_ref, x1_ref, h2_ref, s_ref):
    ya = _dot(ga_ref[...], wrnn_ref[...])
    ob = jnp.concatenate([o_ref[hd] for hd in range(N_HEADS)], axis=-1)
    yb = _dot(ob, wattn_ref[...])
    mixed = (gma_ref[...] * ya + gmb_ref[...] * yb).astype(BF16)
    x1 = x_ref[...] + mod_ref[2:3, :] * _dot(mixed, wo_ref[...])
    x1_ref[...] = x1
    h2 = _rms(x1, g_ref[...]) * (1.0 + mod_ref[4:5, :]) + mod_ref[3:4, :]
    h2_hi, h2_lo = _split_bf16(h2)
    h2_ref[...] = h2_hi
    logits = (_dot_nt(wrh_ref[...], h2_hi) + _dot_nt(wrh_ref[...], h2_lo)
              + _dot_nt(wrl_ref[...], h2_hi))
    s_ref[...] = _sigmoid(logits)


def _merge(ga, o, gm_a, gm_b, x, mod, g, w_rnn_bf, w_attn_bf, w_o_bf, wr_hi, wr_lo, tm):
    b, l, d = x.shape
    row = pl.BlockSpec((None, tm, d), lambda bi, i: (bi, i, 0))
    wspec = pl.BlockSpec((d, d), lambda bi, i: (0, 0), pipeline_mode=pl.Buffered(1))
    rspec = pl.BlockSpec((N_EXPERTS, d), lambda bi, i: (0, 0))
    return pl.pallas_call(
        _merge_kernel,
        out_shape=(jax.ShapeDtypeStruct((b, l, d), F32),
                   jax.ShapeDtypeStruct((b, l, d), BF16),
                   jax.ShapeDtypeStruct((b, N_EXPERTS, l), F32)),
        grid=(b, l // tm),
        in_specs=[row,
                  pl.BlockSpec((None, N_HEADS, tm, V_DIM), lambda bi, i: (bi, 0, i, 0)),
                  row, row, row,
                  pl.BlockSpec((None, 6, d), lambda bi, i: (bi, 0, 0)),
                  pl.BlockSpec((1, d), lambda bi, i: (0, 0)),
                  wspec, wspec, wspec, rspec, rspec],
        out_specs=(row, row, pl.BlockSpec((None, N_EXPERTS, tm), lambda bi, i: (bi, 0, i))),
        compiler_params=_cparams("parallel", "parallel"),
        name="merge_router",
    )(ga, o, gm_a, gm_b, x, mod, g, w_rnn_bf, w_attn_bf, w_o_bf, wr_hi, wr_lo)


def _route_kernel(s_ref, rb_ref, w_ref):
    tr = s_ref.shape[1]
    neg = -jnp.inf
    sub = lax.broadcasted_iota(jnp.int32, (GROUP_SIZE, tr), 0)
    s_g = [s_ref[g * GROUP_SIZE:(g + 1) * GROUP_SIZE, :] for g in range(N_GROUPS)]
    sb_g = [s_g[g] + rb_ref[g * GROUP_SIZE:(g + 1) * GROUP_SIZE, :] for g in range(N_GROUPS)]

    gscore = []
    for g in range(N_GROUPS):
        xg = sb_g[g]
        top1 = jnp.max(xg, axis=0, keepdims=True)
        first = jnp.min(jnp.where(xg == top1, sub, GROUP_SIZE), axis=0, keepdims=True)
        top2 = jnp.max(jnp.where(sub == first, neg, xg), axis=0, keepdims=True)
        gscore.append(top1 + top2)

    masked = []
    for g in range(N_GROUPS):
        rank = jnp.zeros((1, tr), jnp.int32)
        for g2 in range(N_GROUPS):
            if g2 == g:
                continue
            ahead = (gscore[g2] >= gscore[g]) if g2 < g else (gscore[g2] > gscore[g])
            rank = rank + ahead.astype(jnp.int32)
        keep = jnp.broadcast_to(rank < TOPK_GROUPS, (GROUP_SIZE, tr))
        masked.append(jnp.where(keep, sb_g[g], neg))

    ranks = [jnp.zeros((GROUP_SIZE, tr), jnp.int32) for _ in range(N_GROUPS)]
    for g2 in range(N_GROUPS):
        for r2 in range(GROUP_SIZE):
            other = jnp.broadcast_to(masked[g2][r2:r2 + 1, :], (GROUP_SIZE, tr))
            for g in range(N_GROUPS):
                mine = masked[g]
                if g2 < g:
                    ahead = other >= mine
                elif g2 > g:
                    ahead = other > mine
                else:
                    ahead = (other > mine) | ((other == mine) & (sub > r2))
                ranks[g] = ranks[g] + ahead.astype(jnp.int32)

    picked = [jnp.where(ranks[g] < TOP_K, s_g[g], 0.0) for g in range(N_GROUPS)]
    total = picked[0]
    for g in range(1, N_GROUPS):
        total = total + picked[g]
    denom = jnp.sum(total, axis=0, keepdims=True)
    for g in range(N_GROUPS):
        w_ref[g * GROUP_SIZE:(g + 1) * GROUP_SIZE, :] = picked[g] / denom * ROUTED_SCALE
    w_ref[N_EXPERTS:N_EXPERTS + GROUP_SIZE, :] = jnp.where(sub == 0, 1.0, 0.0)
    w_ref[N_EXPERTS + GROUP_SIZE:, :] = jnp.zeros((WEIGHT_COLS - N_EXPERTS - GROUP_SIZE, tr), F32)


def _route(s_t, router_bias_col, tr):
    b, ne, l = s_t.shape
    return pl.pallas_call(
        _route_kernel,
        out_shape=jax.ShapeDtypeStruct((b, WEIGHT_COLS, l), F32),
        grid=(b, l // tr),
        in_specs=[pl.BlockSpec((None, ne, tr), lambda bi, i: (bi, 0, i)),
                  pl.BlockSpec((ne, 1), lambda bi, i: (0, 0))],
        out_specs=pl.BlockSpec((None, WEIGHT_COLS, tr), lambda bi, i: (bi, 0, i)),
        compiler_params=_cparams("parallel", "parallel"),
        name="route_topk",
    )(s_t, router_bias_col)


def _moe_kernel(h_ref, w_ref, x1_ref, mod_ref, g_ref, w1_ref, w3_ref, w2_ref, w2p_ref,
                w1s_ref, w3s_ref, w2s_ref, o_ref, acc_ref, hid_ref, *, nsteps, rc):
    st = pl.program_id(2)
    tm = h_ref.shape[0]
    eps = MOE_EXPERTS_PER_STEP
    chunks = [slice(r * rc, (r + 1) * rc) for r in range(tm // rc)]

    @pl.when(st == 0)
    def _():
        acc_ref[...] = jnp.zeros_like(acc_ref)
        hid_ref[...] = jnp.zeros_like(hid_ref)

    w = w_ref[...]
    lane = lax.broadcasted_iota(jnp.int32, w.shape, 1)

    def up_chunk(slot, w1, w3, wcol, rows):
        h = h_ref[rows, :]
        a1 = _dot(h, w1)
        hidden = (a1 * _sigmoid(a1)) * _dot(h, w3)
        if wcol is not None:
            hidden = hidden * wcol[rows, :]
        hid_ref[slot % 2, rows, :] = hidden.astype(BF16)

    def down_chunk(slot, w2, rows):
        acc_ref[rows, :] += _dot(hid_ref[slot % 2, rows, :], w2)

    for s in range(eps):
        wcol = jnp.sum(jnp.where(lane == st * eps + s, w, 0.0), axis=-1, keepdims=True)
        w2_prev = w2p_ref[...] if s == 0 else w2_ref[s - 1]
        for rows in chunks:
            up_chunk(s, w1_ref[s], w3_ref[s], wcol, rows)
            down_chunk(s - 1, w2_prev, rows)

    @pl.when(st == nsteps - 1)
    def _():
        for rows in chunks:
            up_chunk(eps, w1s_ref[...], w3s_ref[...], None, rows)
            down_chunk(eps - 1, w2_ref[eps - 1], rows)
        for rows in chunks:
            down_chunk(eps, w2s_ref[...], rows)
        y = x1_ref[...] + mod_ref[5:6, :] * acc_ref[...]
        o_ref[...] = _rms(y, g_ref[...])


def _moe(h2, w_tok, x1, mod, final_g, w1_all, w3_all, w2_all, w1_s, w3_s, w2_s, tm, rc):
    b, l, d = x1.shape
    eps = MOE_EXPERTS_PER_STEP
    assert eps % 2 == 0 and w1_all.shape[0] % eps == 0
    nsteps = w1_all.shape[0] // eps
    row = pl.BlockSpec((None, tm, d), lambda bi, i, e: (bi, i, 0), pipeline_mode=pl.Buffered(1))
    up_s = pl.BlockSpec((d, D_EXPERT), lambda bi, i, e: (0, 0), pipeline_mode=pl.Buffered(1))
    return pl.pallas_call(
        functools.partial(_moe_kernel, nsteps=nsteps, rc=rc),
        out_shape=jax.ShapeDtypeStruct((b, l, d), F32),
        grid=(b, l // tm, nsteps),
        in_specs=[pl.BlockSpec((None, tm, d), lambda bi, i, e: (bi, i, 0)),
                  pl.BlockSpec((None, tm, WEIGHT_COLS), lambda bi, i, e: (bi, i, 0)),
                  row,
                  pl.BlockSpec((None, 6, d), lambda bi, i, e: (bi, 0, 0)),
                  pl.BlockSpec((1, d), lambda bi, i, e: (0, 0)),
                  pl.BlockSpec((eps, d, D_EXPERT), lambda bi, i, e: (e, 0, 0)),
                  pl.BlockSpec((eps, d, D_EXPERT), lambda bi, i, e: (e, 0, 0)),
                  pl.BlockSpec((eps, D_EXPERT, d), lambda bi, i, e: (e, 0, 0)),
                  pl.BlockSpec((None, D_EXPERT, d), lambda bi, i, e: (jnp.maximum(e * eps - 1, 0), 0, 0)),
                  up_s, up_s,
                  pl.BlockSpec((D_EXPERT, d), lambda bi, i, e: (0, 0), pipeline_mode=pl.Buffered(1))],
        out_specs=pl.BlockSpec((None, tm, d), lambda bi, i, e: (bi, i, 0)),
        scratch_shapes=[pltpu.VMEM((tm, d), F32), pltpu.VMEM((2, tm, D_EXPERT), BF16)],
        compiler_params=_cparams("parallel", "parallel", "arbitrary"),
        name="moe_experts",
    )(h2, w_tok, x1, mod, final_g, w1_all, w3_all, w2_all, w2_all, w1_s, w3_s, w2_s)


def _t5_bucket_np(rel):
    half = N_BUCKETS // 2
    max_exact = half // 2
    ret = np.where(rel > 0, half, 0)
    n = np.abs(rel)
    nf = np.maximum(n, 1).astype(np.float32)
    large = max_exact + (np.log(nf / np.float32(max_exact)) / np.float32(math.log(MAX_DISTANCE / max_exact))
                         * np.float32(half - max_exact)).astype(np.int32)
    large = np.minimum(large, half - 1)
    return (ret + np.where(n < max_exact, n, large)).astype(np.int32)


def _bias_tables(rel_bias, t):
    assert t >= MAX_DISTANCE
    v = np.arange(2 * t)
    d = np.where(v < t, v, v - 2 * t)
    rel = np.stack([off - d for off in (-t, 0, t)])
    scaled = rel_bias.astype(F32) * LOG2E
    z = jnp.take(scaled, jnp.asarray(_t5_bucket_np(rel)), axis=0)
    z = jnp.transpose(z, (2, 0, 1))[:, :, None, :]
    far = jnp.stack([scaled[int(_t5_bucket_np(np.array(-MAX_DISTANCE)))],
                     scaled[int(_t5_bucket_np(np.array(MAX_DISTANCE)))]])
    return z, far


def _block_diag_gates(w_r, w_i):
    per = GATE_CHUNK // LRU_BLOCK
    eye = jnp.eye(per, dtype=F32)

    def bd(w):
        w = w.reshape(-1, per, LRU_BLOCK, LRU_BLOCK)
        return jnp.einsum('cjab,jk->cjakb', w, eye).reshape(-1, GATE_CHUNK, GATE_CHUNK)

    return jnp.concatenate([bd(w_r), bd(w_i)], axis=-1).astype(BF16)


def _tile(l, pref):
    return min(l, pref)


def _trunk(x, mod, p, tiles=None):
    b, l, d = x.shape
    tl = dict(inproj=512, scan=256, attn=1024, attn_rows=256, merge=512, route=512, moe=1024)
    if tiles:
        tl.update(tiles)
    tl = {k: _tile(l, v) for k, v in tl.items()}
    tl['attn_rows'] = min(tl['attn_rows'], tl['attn'])

    u_rnn, u_gate, q, k, vt, gm_a, gm_b = _inproj(x, mod, p['norm_mix_g'], p['w_main'], p['w_vt'],
                                                  tl['inproj'])

    h_fwd = _scan(u_rnn, p['conv_w'], p['conv_b'], p['wbd'][0], p['b_r'][0:1], p['b_i'][0:1],
                  p['lam'][0:1], tl['scan'])
    ga = _scan(u_rnn, p['conv_w'], p['conv_b'], p['wbd'][1], p['b_r'][1:2], p['b_i'][1:2],
               p['lam'][1:2], tl['scan'], h_fwd=h_fwd, u_gate=u_gate)

    bias_tiles, far = p['bias'](tl['attn'], tl['attn_rows'])
    o = _attention(q, k, vt, bias_tiles, far, p['lam_params'], p['subln_g'], tl['attn'], tl['attn_rows'])

    x1, h2, s_t = _merge(ga, o, gm_a, gm_b, x, mod, p['norm_ffn_g'], p['w_rnn_out'], p['w_attn_out'],
                         p['w_o'], p['wr_hi'], p['wr_lo'], tl['merge'])
    w_t = _route(s_t, p['router_bias'], tl['route'])
    w_tok = jnp.swapaxes(w_t, 1, 2)
    return _moe(h2, w_tok, x1, mod, p['final_norm_g'], p['w1'], p['w3'], p['w2'],
                p['w1_s'], p['w3_s'], p['w2_s'], tl['moe'], min(tl['moe'], MOE_ROW_CHUNK))


def _prepare(norm_mix_g, norm_ffn_g, final_norm_g, w_in, conv_w, conv_b, w_rgate, b_rgate, w_igate,
             b_igate, lru_lambda, w_rnn_out, lambda_q1, lambda_k1, lambda_q2, lambda_k2, subln_g,
             rel_bias, w_attn_out, w_o, w_router, router_bias, w1_e, w3_e, w2_e, w1_s, w3_s, w2_s):
    wr_t = w_router[0].T.astype(F32)
    wr_hi = wr_t.astype(BF16)
    wr_lo = (wr_t - wr_hi.astype(F32)).astype(BF16)
    bias_cache = {}

    def bias(t, rc):
        if (t, rc) not in bias_cache:
            z, far = _bias_tables(rel_bias, t)
            bias_cache[(t, rc)] = (_bias_tiles(z, t, rc), far)
        return bias_cache[(t, rc)]

    w_in0 = w_in[0]
    v_lo = 2 * D_RNN + 2 * N_HEADS * 2 * HEAD_DIM
    v_hi = v_lo + N_HEADS * V_DIM
    return dict(
        norm_mix_g=norm_mix_g[0][None], norm_ffn_g=norm_ffn_g[0][None], final_norm_g=final_norm_g[None],
        w_main=w_in0.astype(BF16),
        w_vt=w_in0[:, v_lo:v_hi].T.astype(BF16),
        conv_w=conv_w[0], conv_b=conv_b[0][None],
        wbd=jnp.stack([_block_diag_gates(w_rgate[0, dr], w_igate[0, dr]) for dr in range(2)]),
        b_r=b_rgate[0], b_i=b_igate[0], lam=lru_lambda[0],
        w_rnn_out=w_rnn_out[0].astype(BF16), w_attn_out=w_attn_out[0].astype(BF16),
        w_o=w_o[0].astype(BF16),
        lam_params=jnp.stack([lambda_q1[0], lambda_k1[0], lambda_q2[0], lambda_k2[0]]).astype(F32),
        subln_g=subln_g[0][:, None].astype(F32), bias=bias,
        wr_hi=wr_hi, wr_lo=wr_lo, router_bias=router_bias[0][:, None].astype(F32),
        w1=w1_e[0].astype(BF16), w3=w3_e[0].astype(BF16), w2=w2_e[0].astype(BF16),
        w1_s=w1_s[0].astype(BF16), w3_s=w3_s[0].astype(BF16), w2_s=w2_s[0].astype(BF16),
    )


def kernel(x_prompt, x_sample, c_prompt, c_sample, norm_mix_g, norm_ffn_g, final_norm_g, w_ada, b_ada, w_in, conv_w, conv_b, w_rgate, b_rgate, w_igate, b_igate, lru_lambda, w_rnn_out, lambda_q1, lambda_k1, lambda_q2, lambda_k2, subln_g, rel_bias, w_attn_out, w_o, w_router, router_bias, w1_e, w3_e, w2_e, w1_s, w3_s, w2_s):
    d = D_MODEL
    nb_p, nb_s = c_prompt.shape[0], c_sample.shape[0]
    pad = (-(nb_p + nb_s)) % SUBLANES
    c_all = jnp.concatenate([c_prompt, c_sample, jnp.zeros((pad, d), F32)], axis=0)
    mod = _ada(c_all, w_ada[0], b_ada[0][None]).reshape(-1, 6, d)
    p = _prepare(norm_mix_g, norm_ffn_g, final_norm_g, w_in, conv_w, conv_b, w_rgate, b_rgate,
                 w_igate, b_igate, lru_lambda, w_rnn_out, lambda_q1, lambda_k1, lambda_q2, lambda_k2,
                 subln_g, rel_bias, w_attn_out, w_o, w_router, router_bias, w1_e, w3_e, w2_e,
                 w1_s, w3_s, w2_s)
    y_prompt = _trunk(x_prompt, mod[:nb_p], p)
    y_sample = _trunk(x_sample, mod[nb_p:nb_p + nb_s], p)
    return (y_prompt, y_sample)
```

```python
import functools
import math

import numpy as np
import jax
import jax.numpy as jnp
from jax import lax
from jax.experimental import pallas as pl
from jax.experimental.pallas import tpu as pltpu

D_MODEL = 1024
D_RNN = 1024
LRU_BLOCK = 64
LRU_C = 8.0
N_HEADS = 8
HEAD_DIM = 64
V_DIM = 2 * HEAD_DIM
N_BUCKETS = 32
MAX_DISTANCE = 128
N_EXPERTS = 64
TOP_K = 8
N_GROUPS = 8
GROUP_SIZE = N_EXPERTS // N_GROUPS
TOPK_GROUPS = 4
D_EXPERT = 256
ROUTED_SCALE = 2.5
EPS = 1e-6
LAM_INIT = 0.8 - 0.6 * math.exp(-0.3 * 0)

VMEM_LIMIT_BYTES = 56 * 1024 * 1024
LANES = 128
SUBLANES = 8
GATE_CHUNK = 256
MOE_EXPERTS_PER_STEP = 4
MOE_ROW_CHUNK = 256
KEY_BLOCK = 256
ATTN_HEADS_PER_STEP = 4
PIPE_SKEW = 2
S_BUFFERS = 4
V_ROWS = V_DIM + 16
LOG2E = math.log2(math.e)
WEIGHT_COLS = 128

F32 = jnp.float32
BF16 = jnp.bfloat16


def _cparams(*sem):
    return pltpu.CompilerParams(dimension_semantics=sem, vmem_limit_bytes=VMEM_LIMIT_BYTES)


def _dot(a, b):
    return jnp.dot(a, b, preferred_element_type=F32)


def _dot_nt(a, b):
    return lax.dot_general(a, b, (((1,), (1,)), ((), ())), preferred_element_type=F32)


def _split_bf16(x):
    hi = x.astype(BF16)
    lo = (x - hi.astype(F32)).astype(BF16)
    return hi, lo


def _sigmoid(x):
    return 1.0 / (1.0 + jnp.exp(-x))


def _rms(x, g):
    return x * lax.rsqrt(jnp.mean(x * x, axis=-1, keepdims=True) + EPS) * g


def _ada_kernel(c_ref, w_ref, b_ref, o_ref):
    c = c_ref[...]
    sc = c * _sigmoid(c)
    c_hi, c_lo = _split_bf16(sc)
    w_hi, w_lo = _split_bf16(w_ref[...])
    o_ref[...] = _dot(c_hi, w_hi) + _dot(c_lo, w_hi) + _dot(c_hi, w_lo) + b_ref[...]


def _ada(c_all, w_ada, b_ada, tn=1536):
    rows, d = c_all.shape
    n = w_ada.shape[1]
    return pl.pallas_call(
        _ada_kernel,
        out_shape=jax.ShapeDtypeStruct((rows, n), F32),
        grid=(n // tn,),
        in_specs=[pl.BlockSpec((rows, d), lambda j: (0, 0)),
                  pl.BlockSpec((d, tn), lambda j: (0, j)),
                  pl.BlockSpec((1, tn), lambda j: (0, j))],
        out_specs=pl.BlockSpec((rows, tn), lambda j: (0, j)),
        compiler_params=_cparams("parallel"),
        name="ada_mod",
    )(c_all, w_ada, b_ada)


def _inproj_kernel(x_ref, mod_ref, g_ref, w_ref, wvt_ref, urnn_ref, ugate_ref, q_ref, k_ref, vt_ref,
                   ga_ref, gb_ref):
    d = D_MODEL
    x = x_ref[...]
    h = _rms(x, g_ref[...]) * (1.0 + mod_ref[1:2, :]) + mod_ref[0:1, :]
    hb = h.astype(BF16)

    def col(c):
        return _dot(hb, w_ref[:, c * d:(c + 1) * d])

    urnn_ref[...] = col(0)
    ugate_ref[...] = col(1)
    q = (col(2) * (HEAD_DIM ** -0.5 * LOG2E)).astype(BF16)
    k = col(3).astype(BF16)
    vt = _dot_nt(wvt_ref[...], hb).astype(BF16)
    for hd in range(N_HEADS):
        sl = slice(hd * V_DIM, (hd + 1) * V_DIM)
        q_ref[hd] = q[:, sl]
        k_ref[hd] = k[:, sl]
        vt_ref[hd] = vt[sl, :]
    ga_ref[...] = _sigmoid(col(5))
    gb_ref[...] = _sigmoid(col(6))


def _inproj(x, mod, g, w_main_bf, w_vt_bf, tm):
    b, l, d = x.shape
    ncol = w_main_bf.shape[1]
    row = pl.BlockSpec((None, tm, d), lambda bi, i: (bi, i, 0))
    head = pl.BlockSpec((None, N_HEADS, tm, V_DIM), lambda bi, i: (bi, 0, i, 0))
    head_t = pl.BlockSpec((None, N_HEADS, V_DIM, tm), lambda bi, i: (bi, 0, 0, i))
    f32_out = jax.ShapeDtypeStruct((b, l, d), F32)
    head_out = jax.ShapeDtypeStruct((b, N_HEADS, l, V_DIM), BF16)
    head_t_out = jax.ShapeDtypeStruct((b, N_HEADS, V_DIM, l), BF16)
    return pl.pallas_call(
        _inproj_kernel,
        out_shape=(f32_out, f32_out, head_out, head_out, head_t_out, f32_out, f32_out),
        grid=(b, l // tm),
        in_specs=[row,
                  pl.BlockSpec((None, 6, d), lambda bi, i: (bi, 0, 0)),
                  pl.BlockSpec((1, d), lambda bi, i: (0, 0)),
                  pl.BlockSpec((d, ncol), lambda bi, i: (0, 0), pipeline_mode=pl.Buffered(1)),
                  pl.BlockSpec((d, d), lambda bi, i: (0, 0), pipeline_mode=pl.Buffered(1))],
        out_specs=(row, row, head, head, head_t, row, row),
        compiler_params=_cparams("parallel", "parallel"),
        name="in_proj",
    )(x, mod, g, w_main_bf, w_vt_bf)


def _gelu_tanh(x):
    return x * (0.5 * (1.0 + jnp.tanh(math.sqrt(2.0 / math.pi) * (x + 0.044715 * (x * x * x)))))


def _scan_kernel(*refs, tl, nt, reverse):
    if reverse:
        (u_ref, prev_ref, next_ref, cw_ref, cb_ref, wbd_ref, br_ref, bi_ref, lam_ref,
         hf_ref, ug_ref, o_ref, carry_ref, h_ref) = refs
    else:
        (u_ref, prev_ref, next_ref, cw_ref, cb_ref, wbd_ref, br_ref, bi_ref, lam_ref,
         o_ref, carry_ref, h_ref) = refs
    c = D_RNN
    step = pl.program_id(1)
    t = (nt - 1 - step) if reverse else step

    @pl.when(step == 0)
    def _():
        carry_ref[...] = jnp.zeros_like(carry_ref)

    u = u_ref[...]
    row = lax.broadcasted_iota(jnp.int32, (tl, c), 0)
    halo_lo = jnp.where(t > 0, prev_ref[...], 0.0)
    halo_hi = jnp.where(t < nt - 1, next_ref[...], 0.0)
    xc = cb_ref[...] + cw_ref[2:3, :] * u
    for k in (2, 1):
        lo_fill = jnp.concatenate([pltpu.roll(halo_lo, k, 0)] + [halo_lo] * (tl // SUBLANES - 1), axis=0)
        shifted = jnp.where(row < k, lo_fill, pltpu.roll(u, k, 0))
        xc = xc + cw_ref[2 - k:3 - k, :] * shifted
    hi_fill = jnp.concatenate([halo_hi] * (tl // SUBLANES - 1) + [pltpu.roll(halo_hi, SUBLANES - 1, 0)],
                              axis=0)
    shifted = jnp.where(row >= tl - 1, hi_fill, pltpu.roll(u, tl - 1, 0))
    xc = xc + cw_ref[3:4, :] * shifted

    xcb = xc.astype(BF16)
    r_parts, i_parts = [], []
    for ch in range(c // GATE_CHUNK):
        z = _dot(xcb[:, ch * GATE_CHUNK:(ch + 1) * GATE_CHUNK], wbd_ref[ch])
        r_parts.append(z[:, :GATE_CHUNK])
        i_parts.append(z[:, GATE_CHUNK:])
    r = _sigmoid(jnp.concatenate(r_parts, axis=-1) + br_ref[...])
    ig = _sigmoid(jnp.concatenate(i_parts, axis=-1) + bi_ref[...])
    nl = -lam_ref[...]
    softplus = jnp.maximum(nl, 0.0) + jnp.log1p(jnp.exp(-jnp.abs(nl)))
    log_a = (-LRU_C) * r * softplus
    a = jnp.exp(log_a)
    one_m_a2 = -jnp.tanh(log_a) * (1.0 + a * a)
    root = jnp.where(one_m_a2 > 0.0, one_m_a2 * lax.rsqrt(one_m_a2), 0.0)
    bb = root * (ig * xc)

    ngroups = tl // SUBLANES
    a = a.reshape(ngroups, SUBLANES, c)
    bb = bb.reshape(ngroups, SUBLANES, c)
    pos = lax.broadcasted_iota(jnp.int32, (ngroups, SUBLANES, c), 1)
    s = 1
    while s < SUBLANES:
        if reverse:
            valid = pos < SUBLANES - s
            shift = SUBLANES - s
        else:
            valid = pos >= s
            shift = s
        a_n = jnp.where(valid, pltpu.roll(a, shift, 1), 1.0)
        b_n = jnp.where(valid, pltpu.roll(bb, shift, 1), 0.0)
        bb = a * b_n + bb
        a = a * a_n
        s *= 2
    carry = carry_ref[...]
    for step_g in range(ngroups):
        g = (ngroups - 1 - step_g) if reverse else step_g
        h_g = a[g] * carry + bb[g]
        h_ref[g * SUBLANES:(g + 1) * SUBLANES, :] = h_g
        carry = h_g[0:1, :] if reverse else h_g[SUBLANES - 1:SUBLANES, :]
    carry_ref[...] = carry

    if reverse:
        o_ref[...] = (_gelu_tanh(ug_ref[...]) * (hf_ref[...] + h_ref[...])).astype(o_ref.dtype)
    else:
        o_ref[...] = h_ref[...]


def _scan(u, conv_w, conv_b, wbd, b_r, b_i, lam, tl, h_fwd=None, u_gate=None):
    b, l, c = u.shape
    nt = l // tl
    reverse = h_fwd is not None
    tpb = tl // SUBLANES
    nblk = l // SUBLANES

    def tix(s):
        return (nt - 1 - s) if reverse else s

    row = pl.BlockSpec((None, tl, c), lambda bi, s: (bi, tix(s), 0))
    prev = pl.BlockSpec((None, SUBLANES, c), lambda bi, s: (bi, jnp.maximum(tix(s) * tpb - 1, 0), 0))
    nxt = pl.BlockSpec((None, SUBLANES, c),
                       lambda bi, s: (bi, jnp.minimum((tix(s) + 1) * tpb, nblk - 1), 0))
    vec = pl.BlockSpec((1, c), lambda bi, s: (0, 0))
    in_specs = [row, prev, nxt,
                pl.BlockSpec((4, c), lambda bi, s: (0, 0)), vec,
                pl.BlockSpec(wbd.shape, lambda bi, s: (0, 0, 0)), vec, vec, vec]
    args = [u, u, u, conv_w, conv_b, wbd, b_r, b_i, lam]
    if reverse:
        in_specs += [row, row]
        args += [h_fwd, u_gate]
        out_dtype = BF16
    else:
        out_dtype = F32
    return pl.pallas_call(
        functools.partial(_scan_kernel, tl=tl, nt=nt, reverse=reverse),
        out_shape=jax.ShapeDtypeStruct((b, l, c), out_dtype),
        grid=(b, nt),
        in_specs=in_specs,
        out_specs=row,
        scratch_shapes=[pltpu.VMEM((1, c), F32), pltpu.VMEM((tl, c), F32)],
        compiler_params=_cparams("parallel", "arbitrary"),
        name="rglru_bwd" if reverse else "rglru_fwd",
    )(*args)


def _bias_kernel(z_ref, o_ref, *, t, rc):
    y = jnp.broadcast_to(z_ref[...], (LANES, 2 * t))
    row = lax.broadcasted_iota(jnp.int32, (LANES, 2 * t), 0)
    bit = 0
    while (1 << bit) < LANES:
        y = jnp.where(((row >> bit) & 1) == 1, pltpu.roll(y, 1 << bit, 1), y)
        bit += 1
    for rh in range(t // LANES):
        blk = pltpu.roll(y, LANES * rh, 1) if rh else y
        for c in range(t // rc):
            o_ref[c, rh * LANES:(rh + 1) * LANES, :] = blk[:, c * rc:(c + 1) * rc]


def _bias_tiles(z, t, rc):
    nh = z.shape[0]
    return pl.pallas_call(
        functools.partial(_bias_kernel, t=t, rc=rc),
        out_shape=jax.ShapeDtypeStruct((nh, 3, t // rc, t, rc), F32),
        grid=(nh, 3),
        in_specs=[pl.BlockSpec((None, None, 1, 2 * t), lambda h, o: (h, o, 0, 0))],
        out_specs=pl.BlockSpec((None, None, t // rc, t, rc), lambda h, o: (h, o, 0, 0, 0)),
        compiler_params=_cparams("parallel", "parallel"),
        name="bias_tiles",
    )(z)


def _reduce_row_groups(s, op):
    parts = [s[r:r + SUBLANES, :] for r in range(0, s.shape[0], SUBLANES)]
    while len(parts) > 1:
        parts = [op(a, b) for a, b in zip(parts[0::2], parts[1::2])] + \
                ([parts[-1]] if len(parts) % 2 else [])
    return parts[0]


def _attn_kernel(far_ref, q_ref, k_ref, vt_ref, bias_ref, lamp_ref, g_ref, o_ref,
                 qz_ref, vta_ref, m_ref, alpha_ref, acc_ref, s_ref, *, nk, rc):
    hp = pl.program_id(1)
    i = pl.program_id(2)
    j = pl.program_id(3)
    nhs, t = q_ref.shape[0], q_ref.shape[1]
    nc = t // rc

    nunits = nhs * 2 * nc
    nkb = t // KEY_BLOCK
    assert nunits % S_BUFFERS == 0 and PIPE_SKEW < S_BUFFERS
    cur = j % 2
    prev = 1 - cur

    def unit(u):
        hh, rem = divmod(u, 2 * nc)
        c, br = divmod(rem, 2)
        return hh, c, br

    @pl.when(j == 0)
    def _():
        q = q_ref[...]
        lane = lax.broadcasted_iota(jnp.int32, q.shape, 2)
        qz_ref[0] = jnp.where(lane < HEAD_DIM, q, jnp.zeros_like(q))
        qz_ref[1] = jnp.where(lane >= HEAD_DIM, q, jnp.zeros_like(q))
        m_ref[...] = jnp.full_like(m_ref, -jnp.inf)
        alpha_ref[...] = jnp.ones_like(alpha_ref)
        acc_ref[...] = jnp.zeros_like(acc_ref)
        for u in range(nunits - PIPE_SKEW, nunits):
            s_ref[u % S_BUFFERS] = jnp.zeros(s_ref.shape[1:], F32)
        ones_row = lax.broadcasted_iota(jnp.int32, (V_ROWS - V_DIM, t), 0) == 0
        for side in range(2):
            for hh in range(nhs):
                vta_ref[side, hh, 0:V_DIM, :] = jnp.zeros((V_DIM, t), BF16)
                vta_ref[side, hh, V_DIM:, :] = jnp.where(ones_row, 1.0, 0.0).astype(BF16)

    for hh in range(nhs):
        vta_ref[cur, hh, 0:V_DIM, :] = vt_ref[hh]

    enter_band = j == jnp.maximum(i - 1, 0)
    leave_band = j == jnp.minimum(i + 1, nk - 1) + 1
    shift = [jnp.where(enter_band, far_ref[0, hp * nhs + hh], 0.0)
             - jnp.where(leave_band, far_ref[1, hp * nhs + hh], 0.0) for hh in range(nhs)]

    def value_stage_begin(u, carried):
        hh, c, br = unit(u)
        m_u = m_ref[hh, br, c]
        if carried:
            m_u = jnp.where(j > 0, m_u, jnp.inf)
        return m_u

    def value_stage_block(u, m_u, side, ks, accp):
        p = jnp.exp2(s_ref[u % S_BUFFERS, ks, :] - m_u)
        pv = _dot(vta_ref[side, unit(u)[0], :, ks], p.astype(BF16))
        return pv if accp is None else accp + pv

    def value_stage_end(u, accp):
        hh, c, br = unit(u)
        acc_ref[hh, br, c] = alpha_ref[hh, br, c] * acc_ref[hh, br, c] + accp

    def step(with_bias):
        for slot in range(nunits):
            ha, ca, bra = unit(slot)
            qa = qz_ref[bra, ha, ca * rc:(ca + 1) * rc, :]
            colmax = None
            carried = slot < PIPE_SKEW
            ub = slot - PIPE_SKEW + (nunits if carried else 0)
            side = prev if carried else cur
            m_b = value_stage_begin(ub, carried)
            accp = None
            for kb in range(nkb):
                ks = slice(kb * KEY_BLOCK, (kb + 1) * KEY_BLOCK)
                s = _dot_nt(k_ref[ha, ks, :], qa)
                if with_bias:
                    s = s + bias_ref[ha, ca, ks, :]
                s_ref[slot % S_BUFFERS, ks, :] = s
                bmax = _reduce_row_groups(s, jnp.maximum)
                colmax = bmax if colmax is None else jnp.maximum(colmax, bmax)
                accp = value_stage_block(ub, m_b, side, ks, accp)
            m_old = m_ref[ha, bra, ca] + shift[ha]
            m_new = jnp.maximum(m_old, jnp.max(colmax, axis=0, keepdims=True))
            alpha_ref[ha, bra, ca] = jnp.exp2(m_old - m_new)
            m_ref[ha, bra, ca] = m_new
            value_stage_end(ub, accp)

    is_band = jnp.abs(j - i) <= 1

    @pl.when(is_band)
    def _():
        step(True)

    @pl.when(jnp.logical_not(is_band))
    def _():
        step(False)

    @pl.when(j == nk - 1)
    def _():
        for u in range(nunits - PIPE_SKEW, nunits):
            m_u = value_stage_begin(u, False)
            accp = None
            for kb in range(nkb):
                accp = value_stage_block(u, m_u, cur, slice(kb * KEY_BLOCK, (kb + 1) * KEY_BLOCK), accp)
            value_stage_end(u, accp)
        lp = lamp_ref[...]
        lam = (jnp.exp(jnp.sum(lp[0:1, :] * lp[1:2, :], axis=-1, keepdims=True))
               - jnp.exp(jnp.sum(lp[2:3, :] * lp[3:4, :], axis=-1, keepdims=True)) + LAM_INIT)
        for hh in range(nhs):
            for c in range(nc):
                o = (acc_ref[hh, 0, c, 0:V_DIM, :] / acc_ref[hh, 0, c, V_DIM:V_DIM + 1, :]
                     - lam * (acc_ref[hh, 1, c, 0:V_DIM, :] / acc_ref[hh, 1, c, V_DIM:V_DIM + 1, :]))
                o = o * lax.rsqrt(jnp.mean(o * o, axis=0, keepdims=True) + EPS) * g_ref[...]
                o = o * (1.0 - LAM_INIT)
                o_ref[hh, c * rc:(c + 1) * rc, :] = o.T.astype(o_ref.dtype)


def _attention(q, k, vt, bias_tiles, far_bias, lam_params, subln_g_col, t, rc):
    b, nh, l, dv = q.shape
    nq = l // t
    nc = t // rc
    nhs = ATTN_HEADS_PER_STEP
    assert nh % nhs == 0
    qspec = pl.BlockSpec((None, nhs, t, dv), lambda bi, h, i, j: (bi, h, i, 0))
    kspec = pl.BlockSpec((None, nhs, t, dv), lambda bi, h, i, j: (bi, h, j, 0))
    vspec = pl.BlockSpec((None, nhs, dv, t), lambda bi, h, i, j: (bi, h, 0, j))
    return pl.pallas_call(
        functools.partial(_attn_kernel, nk=nq, rc=rc),
        out_shape=jax.ShapeDtypeStruct((b, nh, l, dv), BF16),
        grid=(b, nh // nhs, nq, nq),
        in_specs=[pl.BlockSpec(memory_space=pltpu.SMEM),
                  qspec, kspec, vspec,
                  pl.BlockSpec((nhs, None, nc, t, rc),
                               lambda bi, h, i, j: (h, jnp.clip(j - i, -1, 1) + 1, 0, 0, 0)),
                  pl.BlockSpec((4, HEAD_DIM), lambda bi, h, i, j: (0, 0)),
                  pl.BlockSpec((dv, 1), lambda bi, h, i, j: (0, 0))],
        out_specs=qspec,
        scratch_shapes=[pltpu.VMEM((2, nhs, t, dv), BF16), pltpu.VMEM((2, nhs, V_ROWS, t), BF16),
                        pltpu.VMEM((nhs, 2, nc, 1, rc), F32), pltpu.VMEM((nhs, 2, nc, 1, rc), F32),
                        pltpu.VMEM((nhs, 2, nc, V_ROWS, rc), F32),
                        pltpu.VMEM((S_BUFFERS, t, rc), F32)],
        compiler_params=_cparams("parallel", "parallel", "parallel", "arbitrary"),
        name="diff_attn",
    )(far_bias, q, k, vt, bias_tiles, lam_params, subln_g_col)


def _merge_kernel(ga_ref, o_ref, gma_ref, gmb_ref, x_ref, mod_ref, g_ref, wrnn_ref, wattn_ref,
                  wo_ref, wrh_ref, wrl_ref, x1_ref, h2_ref, s_ref):
    ya = _dot(ga_ref[...], wrnn_ref[...])
    ob = jnp.concatenate([o_ref[hd] for hd in range(N_HEADS)], axis=-1)
    yb = _dot(ob, wattn_ref[...])
    mixed = (gma_ref[...] * ya + gmb_ref[...] * yb).astype(BF16)
    x1 = x_ref[...] + mod_ref[2:3, :] * _dot(mixed, wo_ref[...])
    x1_ref[...] = x1
    h2 = _rms(x1, g_ref[...]) * (1.0 + mod_ref[4:5, :]) + mod_ref[3:4, :]
    h2_hi, h2_lo = _split_bf16(h2)
    h2_ref[...] = h2_hi
    logits = (_dot_nt(wrh_ref[...], h2_hi) + _dot_nt(wrh_ref[...], h2_lo)
              + _dot_nt(wrl_ref[...], h2_hi))
    s_ref[...] = _sigmoid(logits)


def _merge(ga, o, gm_a, gm_b, x, mod, g, w_rnn_bf, w_attn_bf, w_o_bf, wr_hi, wr_lo, tm):
    b, l, d = x.shape
    row = pl.BlockSpec((None, tm, d), lambda bi, i: (bi, i, 0))
    wspec = pl.BlockSpec((d, d), lambda bi, i: (0, 0), pipeline_mode=pl.Buffered(1))
    rspec = pl.BlockSpec((N_EXPERTS, d), lambda bi, i: (0, 0))
    return pl.pallas_call(
        _merge_kernel,
        out_shape=(jax.ShapeDtypeStruct((b, l, d), F32),
                   jax.ShapeDtypeStruct((b, l, d), BF16),
                   jax.ShapeDtypeStruct((b, N_EXPERTS, l), F32)),
        grid=(b, l // tm),
        in_specs=[row,
                  pl.BlockSpec((None, N_HEADS, tm, V_DIM), lambda bi, i: (bi, 0, i, 0)),
                  row, row, row,
                  pl.BlockSpec((None, 6, d), lambda bi, i: (bi, 0, 0)),
                  pl.BlockSpec((1, d), lambda bi, i: (0, 0)),
                  wspec, wspec, wspec, rspec, rspec],
        out_specs=(row, row, pl.BlockSpec((None, N_EXPERTS, tm), lambda bi, i: (bi, 0, i))),
        compiler_params=_cparams("parallel", "parallel"),
        name="merge_router",
    )(ga, o, gm_a, gm_b, x, mod, g, w_rnn_bf, w_attn_bf, w_o_bf, wr_hi, wr_lo)


def _route_kernel(s_ref, rb_ref, w_ref):
    tr = s_ref.shape[1]
    neg = -jnp.inf
    sub = lax.broadcasted_iota(jnp.int32, (GROUP_SIZE, tr), 0)
    s_g = [s_ref[g * GROUP_SIZE:(g + 1) * GROUP_SIZE, :] for g in range(N_GROUPS)]
    sb_g = [s_g[g] + rb_ref[g * GROUP_SIZE:(g + 1) * GROUP_SIZE, :] for g in range(N_GROUPS)]

    gscore = []
    for g in range(N_GROUPS):
        xg = sb_g[g]
        top1 = jnp.max(xg, axis=0, keepdims=True)
        first = jnp.min(jnp.where(xg == top1, sub, GROUP_SIZE), axis=0, keepdims=True)
        top2 = jnp.max(jnp.where(sub == first, neg, xg), axis=0, keepdims=True)
        gscore.append(top1 + top2)

    masked = []
    for g in range(N_GROUPS):
        rank = jnp.zeros((1, tr), jnp.int32)
        for g2 in range(N_GROUPS):
            if g2 == g:
                continue
            ahead = (gscore[g2] >= gscore[g]) if g2 < g else (gscore[g2] > gscore[g])
            rank = rank + ahead.astype(jnp.int32)
        keep = jnp.broadcast_to(rank < TOPK_GROUPS, (GROUP_SIZE, tr))
        masked.append(jnp.where(keep, sb_g[g], neg))

    ranks = [jnp.zeros((GROUP_SIZE, tr), jnp.int32) for _ in range(N_GROUPS)]
    for g2 in range(N_GROUPS):
        for r2 in range(GROUP_SIZE):
            other = jnp.broadcast_to(masked[g2][r2:r2 + 1, :], (GROUP_SIZE, tr))
            for g in range(N_GROUPS):
                mine = masked[g]
                if g2 < g:
                    ahead = other >= mine
                elif g2 > g:
                    ahead = other > mine
                else:
                    ahead = (other > mine) | ((other == mine) & (sub > r2))
                ranks[g] = ranks[g] + ahead.astype(jnp.int32)

    picked = [jnp.where(ranks[g] < TOP_K, s_g[g], 0.0) for g in range(N_GROUPS)]
    total = picked[0]
    for g in range(1, N_GROUPS):
        total = total + picked[g]
    denom = jnp.sum(total, axis=0, keepdims=True)
    for g in range(N_GROUPS):
        w_ref[g * GROUP_SIZE:(g + 1) * GROUP_SIZE, :] = picked[g] / denom * ROUTED_SCALE
    w_ref[N_EXPERTS:N_EXPERTS + GROUP_SIZE, :] = jnp.where(sub == 0, 1.0, 0.0)
    w_ref[N_EXPERTS + GROUP_SIZE:, :] = jnp.zeros((WEIGHT_COLS - N_EXPERTS - GROUP_SIZE, tr), F32)


def _route(s_t, router_bias_col, tr):
    b, ne, l = s_t.shape
    return pl.pallas_call(
        _route_kernel,
        out_shape=jax.ShapeDtypeStruct((b, WEIGHT_COLS, l), F32),
        grid=(b, l // tr),
        in_specs=[pl.BlockSpec((None, ne, tr), lambda bi, i: (bi, 0, i)),
                  pl.BlockSpec((ne, 1), lambda bi, i: (0, 0))],
        out_specs=pl.BlockSpec((None, WEIGHT_COLS, tr), lambda bi, i: (bi, 0, i)),
        compiler_params=_cparams("parallel", "parallel"),
        name="route_topk",
    )(s_t, router_bias_col)


def _moe_kernel(h_ref, w_ref, x1_ref, mod_ref, g_ref, w1_ref, w3_ref, w2_ref, w2p_ref,
                w1s_ref, w3s_ref, w2s_ref, o_ref, acc_ref, hid_ref, *, nsteps, rc):
    st = pl.program_id(2)
    tm = h_ref.shape[0]
    eps = MOE_EXPERTS_PER_STEP
    chunks = [slice(r * rc, (r + 1) * rc) for r in range(tm // rc)]

    @pl.when(st == 0)
    def _():
        acc_ref[...] = jnp.zeros_like(acc_ref)
        hid_ref[...] = jnp.zeros_like(hid_ref)

    w = w_ref[...]
    lane = lax.broadcasted_iota(jnp.int32, w.shape, 1)

    def up_chunk(slot, w1, w3, wcol, rows):
        h = h_ref[rows, :]
        a1 = _dot(h, w1)
        hidden = (a1 * _sigmoid(a1)) * _dot(h, w3)
        if wcol is not None:
            hidden = hidden * wcol[rows, :]
        hid_ref[slot % 2, rows, :] = hidden.astype(BF16)

    def down_chunk(slot, w2, rows):
        acc_ref[rows, :] += _dot(hid_ref[slot % 2, rows, :], w2)

    for s in range(eps):
        wcol = jnp.sum(jnp.where(lane == st * eps + s, w, 0.0), axis=-1, keepdims=True)
        w2_prev = w2p_ref[...] if s == 0 else w2_ref[s - 1]
        for rows in chunks:
            up_chunk(s, w1_ref[s], w3_ref[s], wcol, rows)
            down_chunk(s - 1, w2_prev, rows)

    @pl.when(st == nsteps - 1)
    def _():
        for rows in chunks:
            up_chunk(eps, w1s_ref[...], w3s_ref[...], None, rows)
            down_chunk(eps - 1, w2_ref[eps - 1], rows)
        for rows in chunks:
            down_chunk(eps, w2s_ref[...], rows)
        y = x1_ref[...] + mod_ref[5:6, :] * acc_ref[...]
        o_ref[...] = _rms(y, g_ref[...])


def _moe(h2, w_tok, x1, mod, final_g, w1_all, w3_all, w2_all, w1_s, w3_s, w2_s, tm, rc):
    b, l, d = x1.shape
    eps = MOE_EXPERTS_PER_STEP
    assert eps % 2 == 0 and w1_all.shape[0] % eps == 0
    nsteps = w1_all.shape[0] // eps
    row = pl.BlockSpec((None, tm, d), lambda bi, i, e: (bi, i, 0))
    up_s = pl.BlockSpec((d, D_EXPERT), lambda bi, i, e: (0, 0))
    return pl.pallas_call(
        functools.partial(_moe_kernel, nsteps=nsteps, rc=rc),
        out_shape=jax.ShapeDtypeStruct((b, l, d), F32),
        grid=(b, l // tm, nsteps),
        in_specs=[pl.BlockSpec((None, tm, d), lambda bi, i, e: (bi, i, 0)),
                  pl.BlockSpec((None, tm, WEIGHT_COLS), lambda bi, i, e: (bi, i, 0)),
                  row,
                  pl.BlockSpec((None, 6, d), lambda bi, i, e: (bi, 0, 0)),
                  pl.BlockSpec((1, d), lambda bi, i, e: (0, 0)),
                  pl.BlockSpec((eps, d, D_EXPERT), lambda bi, i, e: (e, 0, 0)),
                  pl.BlockSpec((eps, d, D_EXPERT), lambda bi, i, e: (e, 0, 0)),
                  pl.BlockSpec((eps, D_EXPERT, d), lambda bi, i, e: (e, 0, 0)),
                  pl.BlockSpec((None, D_EXPERT, d), lambda bi, i, e: (jnp.maximum(e * eps - 1, 0), 0, 0)),
                  up_s, up_s, pl.BlockSpec((D_EXPERT, d), lambda bi, i, e: (0, 0))],
        out_specs=row,
        scratch_shapes=[pltpu.VMEM((tm, d), F32), pltpu.VMEM((2, tm, D_EXPERT), BF16)],
        compiler_params=_cparams("parallel", "parallel", "arbitrary"),
        name="moe_experts",
    )(h2, w_tok, x1, mod, final_g, w1_all, w3_all, w2_all, w2_all, w1_s, w3_s, w2_s)


def _t5_bucket_np(rel):
    half = N_BUCKETS // 2
    max_exact = half // 2
    ret = np.where(rel > 0, half, 0)
    n = np.abs(rel)
    nf = np.maximum(n, 1).astype(np.float32)
    large = max_exact + (np.log(nf / np.float32(max_exact)) / np.float32(math.log(MAX_DISTANCE / max_exact))
                         * np.float32(half - max_exact)).astype(np.int32)
    large = np.minimum(large, half - 1)
    return (ret + np.where(n < max_exact, n, large)).astype(np.int32)


def _bias_tables(rel_bias, t):
    assert t >= MAX_DISTANCE
    v = np.arange(2 * t)
    d = np.where(v < t, v, v - 2 * t)
    rel = np.stack([off - d for off in (-t, 0, t)])
    scaled = rel_bias.astype(F32) * LOG2E
    z = jnp.take(scaled, jnp.asarray(_t5_bucket_np(rel)), axis=0)
    z = jnp.transpose(z, (2, 0, 1))[:, :, None, :]
    far = jnp.stack([scaled[int(_t5_bucket_np(np.array(-MAX_DISTANCE)))],
                     scaled[int(_t5_bucket_np(np.array(MAX_DISTANCE)))]])
    return z, far


def _block_diag_gates(w_r, w_i):
    per = GATE_CHUNK // LRU_BLOCK
    eye = jnp.eye(per, dtype=F32)

    def bd(w):
        w = w.reshape(-1, per, LRU_BLOCK, LRU_BLOCK)
        return jnp.einsum('cjab,jk->cjakb', w, eye).reshape(-1, GATE_CHUNK, GATE_CHUNK)

    return jnp.concatenate([bd(w_r), bd(w_i)], axis=-1).astype(BF16)


def _tile(l, pref):
    return min(l, pref)


def _trunk(x, mod, p, tiles=None):
    b, l, d = x.shape
    tl = dict(inproj=512, scan=512, attn=1024, attn_rows=256, merge=512, route=512, moe=1024)
    if tiles:
        tl.update(tiles)
    tl = {k: _tile(l, v) for k, v in tl.items()}
    tl['attn_rows'] = min(tl['attn_rows'], tl['attn'])

    u_rnn, u_gate, q, k, vt, gm_a, gm_b = _inproj(x, mod, p['norm_mix_g'], p['w_main'], p['w_vt'],
                                                  tl['inproj'])

    h_fwd = _scan(u_rnn, p['conv_w'], p['conv_b'], p['wbd'][0], p['b_r'][0:1], p['b_i'][0:1],
                  p['lam'][0:1], tl['scan'])
    ga = _scan(u_rnn, p['conv_w'], p['conv_b'], p['wbd'][1], p['b_r'][1:2], p['b_i'][1:2],
               p['lam'][1:2], tl['scan'], h_fwd=h_fwd, u_gate=u_gate)

    bias_tiles, far = p['bias'](tl['attn'], tl['attn_rows'])
    o = _attention(q, k, vt, bias_tiles, far, p['lam_params'], p['subln_g'], tl['attn'], tl['attn_rows'])

    x1, h2, s_t = _merge(ga, o, gm_a, gm_b, x, mod, p['norm_ffn_g'], p['w_rnn_out'], p['w_attn_out'],
                         p['w_o'], p['wr_hi'], p['wr_lo'], tl['merge'])
    w_t = _route(s_t, p['router_bias'], tl['route'])
    w_tok = jnp.swapaxes(w_t, 1, 2)
    return _moe(h2, w_tok, x1, mod, p['final_norm_g'], p['w1'], p['w3'], p['w2'],
                p['w1_s'], p['w3_s'], p['w2_s'], tl['moe'], min(tl['moe'], MOE_ROW_CHUNK))


def _prepare(norm_mix_g, norm_ffn_g, final_norm_g, w_in, conv_w, conv_b, w_rgate, b_rgate, w_igate,
             b_igate, lru_lambda, w_rnn_out, lambda_q1, lambda_k1, lambda_q2, lambda_k2, subln_g,
             rel_bias, w_attn_out, w_o, w_router, router_bias, w1_e, w3_e, w2_e, w1_s, w3_s, w2_s):
    wr_t = w_router[0].T.astype(F32)
    wr_hi = wr_t.astype(BF16)
    wr_lo = (wr_t - wr_hi.astype(F32)).astype(BF16)
    bias_cache = {}

    def bias(t, rc):
        if (t, rc) not in bias_cache:
            z, far = _bias_tables(rel_bias, t)
            bias_cache[(t, rc)] = (_bias_tiles(z, t, rc), far)
        return bias_cache[(t, rc)]

    w_in0 = w_in[0]
    v_lo = 2 * D_RNN + 2 * N_HEADS * 2 * HEAD_DIM
    v_hi = v_lo + N_HEADS * V_DIM
    return dict(
        norm_mix_g=norm_mix_g[0][None], norm_ffn_g=norm_ffn_g[0][None], final_norm_g=final_norm_g[None],
        w_main=w_in0.astype(BF16),
        w_vt=w_in0[:, v_lo:v_hi].T.astype(BF16),
        conv_w=conv_w[0], conv_b=conv_b[0][None],
        wbd=jnp.stack([_block_diag_gates(w_rgate[0, dr], w_igate[0, dr]) for dr in range(2)]),
        b_r=b_rgate[0], b_i=b_igate[0], lam=lru_lambda[0],
        w_rnn_out=w_rnn_out[0].astype(BF16), w_attn_out=w_attn_out[0].astype(BF16),
        w_o=w_o[0].astype(BF16),
        lam_params=jnp.stack([lambda_q1[0], lambda_k1[0], lambda_q2[0], lambda_k2[0]]).astype(F32),
        subln_g=subln_g[0][:, None].astype(F32), bias=bias,
        wr_hi=wr_hi, wr_lo=wr_lo, router_bias=router_bias[0][:, None].astype(F32),
        w1=w1_e[0].astype(BF16), w3=w3_e[0].astype(BF16), w2=w2_e[0].astype(BF16),
        w1_s=w1_s[0].astype(BF16), w3_s=w3_s[0].astype(BF16), w2_s=w2_s[0].astype(BF16),
    )


def kernel(x_prompt, x_sample, c_prompt, c_sample, norm_mix_g, norm_ffn_g, final_norm_g, w_ada, b_ada, w_in, conv_w, conv_b, w_rgate, b_rgate, w_igate, b_igate, lru_lambda, w_rnn_out, lambda_q1, lambda_k1, lambda_q2, lambda_k2, subln_g, rel_bias, w_attn_out, w_o, w_router, router_bias, w1_e, w3_e, w2_e, w1_s, w3_s, w2_s):
    d = D_MODEL
    nb_p, nb_s = c_prompt.shape[0], c_sample.shape[0]
    pad = (-(nb_p + nb_s)) % SUBLANES
    c_all = jnp.concatenate([c_prompt, c_sample, jnp.zeros((pad, d), F32)], axis=0)
    mod = _ada(c_all, w_ada[0], b_ada[0][None]).reshape(-1, 6, d)
    p = _prepare(norm_mix_g, norm_ffn_g, final_norm_g, w_in, conv_w, conv_b, w_rgate, b_rgate,
                 w_igate, b_igate, lru_lambda, w_rnn_out, lambda_q1, lambda_k1, lambda_q2, lambda_k2,
                 subln_g, rel_bias, w_attn_out, w_o, w_router, router_bias, w1_e, w3_e, w2_e,
                 w1_s, w3_s, w2_s)
    y_prompt = _trunk(x_prompt, mod[:nb_p], p)
    y_sample = _trunk(x_sample, mod[nb_p:nb_p + nb_s], p)
    return (y_prompt, y_sample)
```
